```python
import math
import jax, jax.numpy as jnp
from jax import lax
import numpy as np

D_MODEL = 1024
BATCH = 2
SEQ = 8192
DEPTH = 2

SSM_WIDTH = 512
SSM_GROUP = 16
SSM_GROUPS = SSM_WIDTH // SSM_GROUP
SSM_STATE = 64
LOG_DT_MIN = math.log(1e-3)
LOG_DT_MAX = math.log(1e-1)
ATTN_HEADS = 8
HEAD_DIM = 64
ATTN_WIDTH = ATTN_HEADS * HEAD_DIM
Q_BLOCK = 128
D_FF = 2816
FFN_RES = 0.5
N_SUB = 3
RMS_EPS = 1e-6
IN_WIDTH = SSM_WIDTH + 3 * ATTN_WIDTH + ATTN_HEADS + 2 * D_MODEL

kernel_name = "hybrid_s5_fox_macaron_adaln"


def rms_norm(x, g):
    xf = x.astype(jnp.float32)
    xf = xf * lax.rsqrt(jnp.mean(xf * xf, axis=-1, keepdims=True) + RMS_EPS)
    return (xf * g.astype(jnp.float32)).astype(x.dtype)


def swiglu(h, w_in, w_out):
    gate, up = jnp.split(h @ w_in, 2, axis=-1)
    return (jax.nn.silu(gate) * up) @ w_out


def s5_ssm(u, a_re, a_im, log_dt, b_re, b_im, c_re, c_im, d_skip):
    f32 = jnp.float32
    bsz, L, _ = u.shape
    uf = u.astype(f32).reshape(bsz, L, SSM_GROUPS, SSM_GROUP)
    lam = lax.complex(jnp.minimum(a_re.astype(f32), -1e-4), a_im.astype(f32))
    dt = jnp.exp(log_dt.astype(f32))[:, None]
    lam_bar = jnp.exp(lam * dt)
    b = lax.complex(b_re.astype(f32), b_im.astype(f32))
    b_bar = ((lam_bar - 1.0) / lam)[..., None] * b
    bu = jnp.einsum('blgn,gpn->blgp', uf.astype(jnp.complex64), b_bar)
    a_all = jnp.broadcast_to(lam_bar, bu.shape)

    def combine(e1, e2):
        a1, s1 = e1
        a2, s2 = e2
        return a2 * a1, a2 * s1 + s2

    _, states = lax.associative_scan(combine, (a_all, bu), axis=1)
    c = lax.complex(c_re.astype(f32), c_im.astype(f32))
    y = jnp.real(jnp.einsum('gnp,blgp->blgn', c, states))
    y = y + d_skip.astype(f32).reshape(SSM_GROUPS, SSM_GROUP) * uf
    return y.reshape(bsz, L, SSM_WIDTH).astype(u.dtype)


def forgetting_attention(q, k, v, f_logit):
    f32 = jnp.float32
    bsz, L, H, Dh = q.shape
    nb = L // Q_BLOCK
    log_f = jax.nn.log_sigmoid(f_logit.astype(f32))
    cum = jnp.cumsum(log_f, axis=1).transpose(0, 2, 1)
    kt = k.transpose(0, 2, 1, 3)
    vt = v.transpose(0, 2, 1, 3)
    qb = q.transpose(0, 2, 1, 3).reshape(bsz, H, nb, Q_BLOCK, Dh).transpose(2, 0, 1, 3, 4)
    cqb = cum.reshape(bsz, H, nb, Q_BLOCK).transpose(2, 0, 1, 3)
    starts = jnp.arange(nb, dtype=jnp.int32) * Q_BLOCK
    k_pos = jnp.arange(L, dtype=jnp.int32)
    scale = Dh ** -0.5

    def one_block(args):
        q_blk, cq_blk, start = args
        s = jnp.einsum('bhqd,bhkd->bhqk', q_blk, kt).astype(f32) * scale
        s = s + (cq_blk[..., None] - cum[:, :, None, :])
        q_pos = start + jnp.arange(Q_BLOCK, dtype=jnp.int32)
        s = jnp.where(k_pos[None, :] <= q_pos[:, None], s, -jnp.inf)
        p = jax.nn.softmax(s, axis=-1).astype(vt.dtype)
        return jnp.einsum('bhqk,bhkd->bhqd', p, vt)

    out = lax.map(one_block, (qb, cqb, starts))
    return out.transpose(1, 0, 3, 2, 4).reshape(bsz, L, H * Dh)


def token_mixer(h, w_in, forget_b, a_re, a_im, log_dt, b_re, b_im, c_re, c_im,
                d_skip, glu_w, attn_w_out, w_out):
    bsz, L, _ = h.shape
    proj = h @ w_in
    cuts = np.cumsum([SSM_WIDTH, ATTN_WIDTH, ATTN_WIDTH, ATTN_WIDTH, ATTN_HEADS, D_MODEL]).tolist()
    u, q, k, v, f, g_a, g_b = jnp.split(proj, cuts, axis=-1)
    y_ssm = s5_ssm(u, a_re, a_im, log_dt, b_re, b_im, c_re, c_im, d_skip)
    z_val, z_gate = jnp.split(jax.nn.gelu(y_ssm) @ glu_w, 2, axis=-1)
    y_a = z_val * jax.nn.sigmoid(z_gate)
    shp = (bsz, L, ATTN_HEADS, HEAD_DIM)
    attn = forgetting_attention(q.reshape(shp), k.reshape(shp), v.reshape(shp), f + forget_b)
    y_b = attn @ attn_w_out
    merged = jax.nn.sigmoid(g_a) * y_a + jax.nn.sigmoid(g_b) * y_b
    return merged @ w_out


def setup_inputs(seed: int = 0) -> dict:
    key = jax.random.key(seed)
    ks = jax.random.split(key, 24)
    f32 = jnp.float32
    nrm = lambda k, shape, s: jax.random.normal(k, shape, f32) * s
    D, G, P, N = D_MODEL, SSM_GROUPS, SSM_STATE, SSM_GROUP
    a_im_init = jnp.pi * jnp.arange(P, dtype=f32)
    return {
        "x": nrm(ks[0], (BATCH, SEQ, D), 1.0),
        "c": nrm(ks[1], (BATCH, D), 1.0),
        "mod_w": nrm(ks[2], (DEPTH, D, N_SUB * 3 * D), 0.5 * D ** -0.5),
        "mod_b": nrm(ks[3], (DEPTH, N_SUB * 3 * D), 0.01),
        "norm_pre": 1.0 + nrm(ks[4], (DEPTH, N_SUB, D), 0.02),
        "norm_post": 1.0 + nrm(ks[5], (DEPTH, N_SUB, D), 0.02),
        "ffn_w_in": nrm(ks[6], (DEPTH, 2, D, 2 * D_FF), D ** -0.5),
        "ffn_w_out": nrm(ks[7], (DEPTH, 2, D_FF, D), D_FF ** -0.5),
        "mix_w_in": nrm(ks[8], (DEPTH, D, IN_WIDTH), D ** -0.5),
        "forget_b": 3.0 + nrm(ks[9], (DEPTH, ATTN_HEADS), 0.5),
        "ssm_a_re": -0.5 + nrm(ks[10], (DEPTH, G, P), 0.01),
        "ssm_a_im": a_im_init + nrm(ks[11], (DEPTH, G, P), 0.01),
        "ssm_log_dt": jax.random.uniform(ks[12], (DEPTH, G), f32, LOG_DT_MIN, LOG_DT_MAX),
        "ssm_b_re": nrm(ks[13], (DEPTH, G, P, N), (2 * N) ** -0.5),
        "ssm_b_im": nrm(ks[14], (DEPTH, G, P, N), (2 * N) ** -0.5),
        "ssm_c_re": nrm(ks[15], (DEPTH, G, N, P), (2 * P) ** -0.5),
        "ssm_c_im": nrm(ks[16], (DEPTH, G, N, P), (2 * P) ** -0.5),
        "ssm_d": nrm(ks[17], (DEPTH, SSM_WIDTH), 1.0),
        "glu_w": nrm(ks[18], (DEPTH, SSM_WIDTH, 2 * D), SSM_WIDTH ** -0.5),
        "attn_w_out": nrm(ks[19], (DEPTH, ATTN_WIDTH, D), ATTN_WIDTH ** -0.5),
        "mix_w_out": nrm(ks[20], (DEPTH, D, D), D ** -0.5),
    }


def reference(x, c, mod_w, mod_b, norm_pre, norm_post, ffn_w_in, ffn_w_out,
              mix_w_in, forget_b, ssm_a_re, ssm_a_im, ssm_log_dt, ssm_b_re,
              ssm_b_im, ssm_c_re, ssm_c_im, ssm_d, glu_w, attn_w_out, mix_w_out):
    bsz = x.shape[0]
    for l in range(DEPTH):
        mod = (jax.nn.silu(c) @ mod_w[l] + mod_b[l]).reshape(bsz, N_SUB, 3, D_MODEL)
        mod = mod[:, :, :, None, :]

        def pre(x_in, i):
            return rms_norm(x_in, norm_pre[l, i]) * (1.0 + mod[:, i, 1]) + mod[:, i, 0]

        def post_add(x_in, y, i, res_w):
            return x_in + res_w * mod[:, i, 2] * rms_norm(y, norm_post[l, i])

        x = post_add(x, swiglu(pre(x, 0), ffn_w_in[l, 0], ffn_w_out[l, 0]), 0, FFN_RES)
        y = token_mixer(pre(x, 1), mix_w_in[l], forget_b[l], ssm_a_re[l], ssm_a_im[l],
                        ssm_log_dt[l], ssm_b_re[l], ssm_b_im[l], ssm_c_re[l], ssm_c_im[l],
                        ssm_d[l], glu_w[l], attn_w_out[l], mix_w_out[l])
        x = post_add(x, y, 1, 1.0)
        x = post_add(x, swiglu(pre(x, 2), ffn_w_in[l, 1], ffn_w_out[l, 1]), 2, FFN_RES)
    return x
```

```python
import functools

import jax
import jax.numpy as jnp
from jax import lax
from jax.experimental import pallas as pl
from jax.experimental.pallas import tpu as pltpu

F32 = jnp.float32
BF16 = jnp.bfloat16

RMS_EPS = 1e-6
FFN_RES = 0.5
A_RE_MAX = -1e-4
LANES = 128
HEAD_PAD = 128
SSM_CHUNK = 16
SCAN_ROWS = 16
VMEM_LIMIT = 56 * 1024 * 1024
HIGHEST = lax.Precision.HIGHEST


def _dot(a, b, precision=None):
    return jnp.dot(a, b, preferred_element_type=F32, precision=precision)


def _rms(x, g):
    ms = jnp.mean(x * x, axis=-1, keepdims=True)
    return x * lax.rsqrt(ms + RMS_EPS) * g


def _silu(x):
    return x * jax.nn.sigmoid(x)


def _params(n_axes, semantics="arbitrary"):
    return pltpu.CompilerParams(dimension_semantics=(semantics,) * n_axes,
                                vmem_limit_bytes=VMEM_LIMIT)


def _const_spec(shape):
    zeros = (0,) * len(shape)
    return pl.BlockSpec(shape, lambda *_: zeros, pipeline_mode=pl.Buffered(1))


def _mod_kernel(c_ref, w_ref, b_ref, o_ref):
    sc = _silu(c_ref[...]).astype(BF16)
    o_ref[0] = _dot(sc, w_ref[0].astype(BF16)) + b_ref[0]


def _mod_call(c, mod_w, mod_b):
    depth, d, n = mod_w.shape
    bsz = c.shape[0]
    rows = 8
    c_pad = jnp.zeros((rows, d), F32).at[:bsz].set(c)
    tn = n // 4 if n % (4 * LANES) == 0 else n
    out = pl.pallas_call(
        _mod_kernel,
        out_shape=jax.ShapeDtypeStruct((depth, rows, n), F32),
        grid=(depth, n // tn),
        in_specs=[pl.BlockSpec((rows, d), lambda l, j: (0, 0)),
                  pl.BlockSpec((1, d, tn), lambda l, j: (l, 0, j)),
                  pl.BlockSpec((1, 1, tn), lambda l, j: (l, 0, j))],
        out_specs=pl.BlockSpec((1, rows, tn), lambda l, j: (l, 0, j)),
        compiler_params=_params(2),
    )(c_pad, mod_w, mod_b.reshape(depth, 1, n))
    return out[:, :bsz]


def _mod_rows(mod_ref, sub):
    shift = mod_ref[0, 3 * sub:3 * sub + 1, :]
    scale = mod_ref[0, 3 * sub + 1:3 * sub + 2, :]
    gate = mod_ref[0, 3 * sub + 2:3 * sub + 3, :]
    return shift, scale, gate


def _ffn_kernel(x_ref, mod_ref, gpre_ref, gpost_ref, wg_ref, wu_ref, wo_ref,
                o_ref, acc_ref, *, sub, tk):
    x = x_ref[...]
    shift, scale, gate = _mod_rows(mod_ref, sub)
    h = (_rms(x, gpre_ref[...]) * (1.0 + scale) + shift).astype(BF16)
    d_ff = wg_ref.shape[1]
    for k in range(d_ff // tk):
        sl = slice(k * tk, (k + 1) * tk)
        g = _dot(h, wg_ref[:, sl])
        u = _dot(h, wu_ref[:, sl])
        a = (_silu(g) * u).astype(BF16)
        contrib = _dot(a, wo_ref[sl, :])
        if k == 0:
            acc_ref[...] = contrib
        else:
            acc_ref[...] += contrib
    o_ref[...] = x + (FFN_RES * gate) * _rms(acc_ref[...], gpost_ref[...])


def _ffn_call(x, mod_l, g_pre, g_post, w_in, w_out, *, sub, seq, tm):
    t, d = x.shape
    d_ff = w_out.shape[0]
    tk = 256 if d_ff % 256 == 0 else d_ff
    w_in = w_in.astype(BF16)
    wg, wu = w_in[:, :d_ff], w_in[:, d_ff:]
    per_b = seq // tm
    row = lambda i: (i, 0)
    return pl.pallas_call(
        functools.partial(_ffn_kernel, sub=sub, tk=tk),
        out_shape=jax.ShapeDtypeStruct((t, d), F32),
        grid=(t // tm,),
        in_specs=[pl.BlockSpec((tm, d), row),
                  pl.BlockSpec((1, 9, d), lambda i: (i // per_b, 0, 0)),
                  _const_spec((1, d)), _const_spec((1, d)),
                  _const_spec((d, d_ff)), _const_spec((d, d_ff)), _const_spec((d_ff, d))],
        out_specs=pl.BlockSpec((tm, d), row),
        scratch_shapes=[pltpu.VMEM((tm, d), F32)],
        compiler_params=_params(1),
    )(x, mod_l, g_pre.reshape(1, d), g_post.reshape(1, d), wg, wu, w_out.astype(BF16))


def _inproj_kernel(x_ref, mod_ref, gpre_ref, wu_ref, wq_ref, wk_ref, wv_ref, wf_ref,
                   wga_ref, wgb_ref, u_ref, q_ref, k_ref, v_ref, f_ref, ga_ref, gb_ref):
    shift, scale, _ = _mod_rows(mod_ref, 1)
    h = (_rms(x_ref[...], gpre_ref[...]) * (1.0 + scale) + shift).astype(BF16)
    u_ref[...] = _dot(h, wu_ref[...]).astype(BF16)
    q_ref[...] = _dot(h, wq_ref[...]).astype(BF16)
    k_ref[...] = _dot(h, wk_ref[...]).astype(BF16)
    v_ref[...] = _dot(h, wv_ref[...]).astype(BF16)
    f_ref[...] = _dot(h, wf_ref[...])
    ga_ref[...] = jax.nn.sigmoid(_dot(h, wga_ref[...])).astype(BF16)
    gb_ref[...] = jax.nn.sigmoid(_dot(h, wgb_ref[...])).astype(BF16)


def _pad_heads(w, heads, dh, scale=1.0):
    d = w.shape[0]
    w = (w * scale).reshape(d, heads, dh)
    w = jnp.pad(w, ((0, 0), (0, 0), (0, HEAD_PAD - dh)))
    return w.reshape(d, heads * HEAD_PAD)


def _inproj_call(x, mod_l, g_pre, w_in, *, heads, dh, ssm_w, seq, tm):
    t, d = x.shape
    aw = heads * dh
    w = w_in
    o = 0
    wu = w[:, o:o + ssm_w]; o += ssm_w
    wq = w[:, o:o + aw]; o += aw
    wk = w[:, o:o + aw]; o += aw
    wv = w[:, o:o + aw]; o += aw
    wf = w[:, o:o + heads]; o += heads
    wga = w[:, o:o + d]; o += d
    wgb = w[:, o:o + d]
    wq = _pad_heads(wq, heads, dh, dh ** -0.5)
    wk = _pad_heads(wk, heads, dh)
    wf = jnp.pad(wf, ((0, 0), (0, LANES - heads)))
    ws = [a.astype(BF16) for a in (wu, wq, wk, wv, wf, wga, wgb)]
    per_b = seq // tm
    row = lambda i: (i, 0)
    hp = heads * HEAD_PAD
    widths = (ssm_w, hp, hp, aw, LANES, d, d)
    dtypes = (BF16, BF16, BF16, BF16, F32, BF16, BF16)
    return pl.pallas_call(
        _inproj_kernel,
        out_shape=[jax.ShapeDtypeStruct((t, n), dt) for n, dt in zip(widths, dtypes)],
        grid=(t // tm,),
        in_specs=[pl.BlockSpec((tm, d), row),
                  pl.BlockSpec((1, 9, d), lambda i: (i // per_b, 0, 0)),
                  _const_spec((1, d))] + [_const_spec(a.shape) for a in ws],
        out_specs=[pl.BlockSpec((tm, n), row) for n in widths],
        compiler_params=_params(1),
    )(x, mod_l, g_pre.reshape(1, d), *ws)


def _split3(c):
    hi = c.astype(BF16).astype(F32)
    r = c - hi
    mid = r.astype(BF16).astype(F32)
    lo = (r - mid).astype(BF16).astype(F32)
    return hi, mid, lo


def _attn_prep_kernel(q_ref, k_ref, f_ref, fb_ref, qo_ref, ko_ref, carry_ref, *, heads, dh):
    tm = f_ref.shape[0]

    @pl.when(pl.program_id(1) == 0)
    def _():
        carry_ref[...] = jnp.zeros_like(carry_ref)

    log_f = jax.nn.log_sigmoid(f_ref[...] + fb_ref[...])
    r = lax.broadcasted_iota(jnp.int32, (tm, tm), 0)
    c = lax.broadcasted_iota(jnp.int32, (tm, tm), 1)
    tri = (c <= r).astype(F32)
    cum = _dot(tri, log_f, HIGHEST) + carry_ref[...]
    carry_ref[...] = cum[tm - 1:tm, :]

    lane = lax.broadcasted_iota(jnp.int32, (tm, HEAD_PAD), 1)
    for h in range(heads):
        hi, mid, lo = _split3(cum[:, h:h + 1])
        one = jnp.ones((tm, 1), F32)
        zero = jnp.zeros((tm, HEAD_PAD), F32)

        def place(vals):
            out = zero
            for j, v in enumerate(vals):
                out = jnp.where(lane == dh + j, v, out)
            return out

        sl = slice(h * HEAD_PAD, (h + 1) * HEAD_PAD)
        q_aug = place((one, one, one, hi, mid, lo))
        k_aug = place((-hi, -mid, -lo, one, one, one))
        qo_ref[:, sl] = (q_ref[:, sl].astype(F32) + q_aug).astype(BF16)
        ko_ref[:, sl] = (k_ref[:, sl].astype(F32) + k_aug).astype(BF16)


def _attn_prep_call(qp, kp, f, forget_b, *, heads, dh, bsz, seq, tm):
    t, hp = qp.shape
    fb = jnp.pad(forget_b.reshape(1, heads), ((0, 0), (0, LANES - heads)))
    per_b = seq // tm
    row = lambda b, i: (b * per_b + i, 0)
    return pl.pallas_call(
        functools.partial(_attn_prep_kernel, heads=heads, dh=dh),
        out_shape=[jax.ShapeDtypeStruct((t, hp), BF16)] * 2,
        grid=(bsz, per_b),
        in_specs=[pl.BlockSpec((tm, hp), row), pl.BlockSpec((tm, hp), row),
                  pl.BlockSpec((tm, LANES), row),
                  pl.BlockSpec((1, LANES), lambda b, i: (0, 0))],
        out_specs=[pl.BlockSpec((tm, hp), row)] * 2,
        scratch_shapes=[pltpu.VMEM((1, LANES), F32)],
        compiler_params=_params(2),
    )(qp, kp, f, fb)


def _attn_kernel(q_ref, k_ref, v_ref, o_ref, *, tq, dh):
    i = pl.program_id(2)
    r = lax.broadcasted_iota(jnp.int32, (tq, tq), 0)
    c = lax.broadcasted_iota(jnp.int32, (tq, tq), 1)
    causal = c <= r
    outs = []
    for hh in range(2):
        hs = slice(hh * HEAD_PAD, (hh + 1) * HEAD_PAD)
        q = q_ref[:, hs]

        def step(j, carry, masked):
            m, l, acc = carry
            rows = pl.ds(pl.multiple_of(j * tq, tq), tq)
            s = lax.dot_general(q, k_ref[rows, hs], (((1,), (1,)), ((), ())),
                                preferred_element_type=F32)
            if masked:
                s = jnp.where(causal, s, -jnp.inf)
            m_new = jnp.maximum(m, jnp.max(s, axis=1, keepdims=True))
            alpha = jnp.exp(m - m_new)
            p = jnp.exp(s - m_new)
            l = alpha * l + jnp.sum(p, axis=1, keepdims=True)
            acc = alpha * acc + _dot(p.astype(BF16), v_ref[rows, :])
            return m_new, l, acc

        init = (jnp.full((tq, 1), -jnp.inf, F32), jnp.zeros((tq, 1), F32),
                jnp.zeros((tq, 2 * dh), F32))
        carry = lax.fori_loop(0, i, lambda j, cr: step(j, cr, False), init)
        _, l, acc = step(i, carry, True)
        outs.append(acc / l)
    lane = lax.broadcasted_iota(jnp.int32, (tq, 2 * dh), 1)
    o_ref[...] = jnp.where(lane < dh, outs[0], outs[1]).astype(BF16)


def _attn_call(q_aug, k_aug, v, *, heads, dh, bsz, seq, tq):
    t = q_aug.shape[0]
    nq = seq // tq
    pair = 2 * HEAD_PAD
    return pl.pallas_call(
        functools.partial(_attn_kernel, tq=tq, dh=dh),
        out_shape=jax.ShapeDtypeStruct((t, heads * dh), BF16),
        grid=(bsz, heads // 2, nq),
        in_specs=[pl.BlockSpec((tq, pair), lambda b, h, i: (b * nq + i, h)),
                  pl.BlockSpec((seq, pair), lambda b, h, i: (b, h)),
                  pl.BlockSpec((seq, 2 * dh), lambda b, h, i: (b, h))],
        out_specs=pl.BlockSpec((tq, 2 * dh), lambda b, h, i: (b * nq + i, h)),
        compiler_params=_params(3),
    )(q_aug, k_aug, v)


def _ssm_prep_kernel(re_c_ref, im_c_ref, re_r_ref, im_r_ref, ldt_ref, ct_re_ref, ct_im_ref,
                     bt_re_ref, bt_im_ref, btt_re_ref, btt_im_ref, d_ref,
                     wi_ref, ws_ref, wo_ref, a1_ref, a2_ref, *, n_ch, p_st):
    cn = SSM_CHUNK * n_ch
    p2 = 2 * p_st
    shift_bits = n_ch.bit_length() - 1
    dt = jnp.exp(ldt_ref[0])

    are_c = jnp.minimum(re_c_ref[0], A_RE_MAX) * dt
    aim_c = im_c_ref[0] * dt
    re_r = jnp.minimum(re_r_ref[0], A_RE_MAX)
    im_r = im_r_ref[0]
    are_r = re_r * dt
    aim_r = im_r * dt

    mag = jnp.exp(are_r)
    lb_re = mag * jnp.cos(aim_r)
    lb_im = mag * jnp.sin(aim_r)
    den = re_r * re_r + im_r * im_r
    xr = lb_re - 1.0
    coef_re = (xr * re_r + lb_im * im_r) / den
    coef_im = (lb_im * re_r - xr * im_r) / den

    tau = lax.shift_right_logical(lax.broadcasted_iota(jnp.int32, (p2, cn), 1), shift_bits)
    top = lax.broadcasted_iota(jnp.int32, (p2, cn), 0) < p_st

    def lam_pow_c(off):
        e = (tau + off).astype(F32)
        pm = jnp.exp(e * are_c)
        pr = pm * jnp.cos(e * aim_c)
        pi = pm * jnp.sin(e * aim_c)
        rr = pr * ct_re_ref[0] - pi * ct_im_ref[0]
        ri = pr * ct_im_ref[0] + pi * ct_re_ref[0]
        return jnp.where(top, rr, -ri)

    wo_ref[0] = lam_pow_c(1).astype(BF16)

    lane_b = lax.broadcasted_iota(jnp.int32, (n_ch, p2), 1) < p_st
    bb_re = coef_re * bt_re_ref[0] - coef_im * bt_im_ref[0]
    bb_im = coef_re * bt_im_ref[0] + coef_im * bt_re_ref[0]
    z = _dot(jnp.where(lane_b, bb_re, bb_im), lam_pow_c(0), HIGHEST)
    zr = lax.broadcasted_iota(jnp.int32, (n_ch, cn), 0)
    zl = lax.broadcasted_iota(jnp.int32, (n_ch, cn), 1)
    z = z + jnp.where(zr == zl, d_ref[0], 0.0)
    for s in range(SSM_CHUNK):
        blk = z if s == 0 else jnp.where(zl >= s * n_ch, pltpu.roll(z, s * n_ch, axis=1), 0.0)
        wi_ref[0, s * n_ch:(s + 1) * n_ch, :] = blk.astype(BF16)

    srow = lax.shift_right_logical(lax.broadcasted_iota(jnp.int32, (cn, p2), 0), shift_bits)
    e = (SSM_CHUNK - 1 - srow).astype(F32)
    pm = jnp.exp(e * are_r)
    pr = pm * jnp.cos(e * aim_r)
    pi = pm * jnp.sin(e * aim_r)
    tb_re = coef_re * btt_re_ref[0] - coef_im * btt_im_ref[0]
    tb_im = coef_re * btt_im_ref[0] + coef_im * btt_re_ref[0]
    lane_s = lax.broadcasted_iota(jnp.int32, (cn, p2), 1) < p_st
    ws_ref[0] = jnp.where(lane_s, pr * tb_re - pi * tb_im, pr * tb_im + pi * tb_re).astype(BF16)

    ar, ai = lb_re, lb_im
    for _ in range(SSM_CHUNK.bit_length() - 1):
        ar, ai = ar * ar - ai * ai, 2.0 * ar * ai
    lane_a = lax.broadcasted_iota(jnp.int32, (1, p2), 1) < p_st
    for k in range(SCAN_ROWS):
        a1_ref[0, k:k + 1, :] = ar
        a2_ref[0, k:k + 1, :] = jnp.where(lane_a, -ai, ai)
        ar, ai = ar * ar - ai * ai, 2.0 * ar * ai


def _ssm_prep_call(a_re, a_im, log_dt, b_re, b_im, c_re, c_im, d_skip):
    g, p_st = a_re.shape
    n_ch = b_re.shape[-1]
    cn = SSM_CHUNK * n_ch
    p2 = 2 * p_st
    col = lambda a: jnp.tile(a[:, :, None], (1, 2, 1))
    rowv = lambda a: jnp.tile(a[:, None, :], (1, 1, 2))
    ct = lambda a: jnp.tile(a.transpose(0, 2, 1), (1, 2, SSM_CHUNK))
    bt = lambda a: jnp.tile(a.transpose(0, 2, 1), (1, 1, 2))
    btt = lambda a: jnp.tile(bt(a), (1, SSM_CHUNK, 1))
    d_pad = jnp.pad(d_skip.reshape(g, 1, n_ch), ((0, 0), (0, 0), (0, cn - n_ch)))
    args = (col(a_re), col(a_im), rowv(a_re), rowv(a_im), log_dt.reshape(g, 1, 1),
            ct(c_re), ct(c_im), bt(b_re), bt(b_im), btt(b_re), btt(b_im), d_pad)
    spec = lambda a: pl.BlockSpec((1,) + a.shape[1:], lambda i: (i, 0, 0))
    out_shapes = [jax.ShapeDtypeStruct((g, cn, cn), BF16),
                  jax.ShapeDtypeStruct((g, cn, p2), BF16),
                  jax.ShapeDtypeStruct((g, p2, cn), BF16),
                  jax.ShapeDtypeStruct((g, SCAN_ROWS, p2), F32),
                  jax.ShapeDtypeStruct((g, SCAN_ROWS, p2), F32)]
    return pl.pallas_call(
        functools.partial(_ssm_prep_kernel, n_ch=n_ch, p_st=p_st),
        out_shape=out_shapes,
        grid=(g,),
        in_specs=[spec(a) for a in args],
        out_specs=[spec(s) for s in out_shapes],
        compiler_params=_params(1),
    )(*args)


def _ssm_kernel(u_ref, wi_ref, ws_ref, wo_ref, a1_ref, a2_ref, y_ref, *, bsz, p_st):
    u = u_ref[0]
    nc = u.shape[0]
    ncb = nc // bsz
    s_all = _dot(u, ws_ref[0])
    row = lax.broadcasted_iota(jnp.int32, (ncb, 2 * p_st), 0)
    for b in range(bsz):
        rows = slice(b * ncb, (b + 1) * ncb)
        x = s_all[rows]
        d, k = 1, 0
        while d < ncb:
            xs = jnp.where(row >= d, pltpu.roll(x, d, axis=0), 0.0)
            x = x + a1_ref[0, k:k + 1, :] * xs + a2_ref[0, k:k + 1, :] * pltpu.roll(xs, p_st, axis=1)
            d, k = 2 * d, k + 1
        s_in = jnp.where(row >= 1, pltpu.roll(x, 1, axis=0), 0.0).astype(BF16)
        y_ref[0, rows, :] = _dot(u[rows], wi_ref[0]) + _dot(s_in, wo_ref[0])


def _ssm_call(u, mats, *, bsz):
    wi, ws, wo, a1, a2 = mats
    g, cn, _ = wi.shape
    p_st = ws.shape[2] // 2
    t, ssm_w = u.shape
    n_ch = ssm_w // g
    nc = t // SSM_CHUNK
    assert (nc // bsz) < 2 ** SCAN_ROWS
    ug = u.reshape(nc, SSM_CHUNK, g, n_ch).transpose(2, 0, 1, 3).reshape(g, nc, cn)
    spec = lambda a: pl.BlockSpec((1,) + a.shape[1:], lambda i: (i, 0, 0))
    y = pl.pallas_call(
        functools.partial(_ssm_kernel, bsz=bsz, p_st=p_st),
        out_shape=jax.ShapeDtypeStruct((g, nc, cn), F32),
        grid=(g,),
        in_specs=[spec(ug), spec(wi), spec(ws), spec(wo), spec(a1), spec(a2)],
        out_specs=pl.BlockSpec((1, nc, cn), lambda i: (i, 0, 0)),
        compiler_params=_params(1),
    )(ug, wi, ws, wo, a1, a2)
    return y.reshape(g, nc, SSM_CHUNK, n_ch).transpose(1, 2, 0, 3).reshape(t, ssm_w)


def _merge_kernel(x_ref, mod_ref, gpost_ref, ys_ref, at_ref, ga_ref, gb_ref,
                  wglu_ref, wat_ref, wout_ref, o_ref):
    d = x_ref.shape[1]
    _, _, gate = _mod_rows(mod_ref, 1)
    z = _dot(jax.nn.gelu(ys_ref[...]).astype(BF16), wglu_ref[...])
    y_a = z[:, :d] * jax.nn.sigmoid(z[:, d:])
    y_b = _dot(at_ref[...], wat_ref[...])
    merged = ga_ref[...].astype(F32) * y_a + gb_ref[...].astype(F32) * y_b
    y = _dot(merged.astype(BF16), wout_ref[...])
    o_ref[...] = x_ref[...] + gate * _rms(y, gpost_ref[...])


def _merge_call(x, mod_l, g_post, y_ssm, attn, ga, gb, glu_w, attn_w_out, w_out, *, seq, tm):
    t, d = x.shape
    per_b = seq // tm
    row = lambda i: (i, 0)
    ws = [glu_w.astype(BF16), attn_w_out.astype(BF16), w_out.astype(BF16)]
    acts = (y_ssm, attn, ga, gb)
    return pl.pallas_call(
        _merge_kernel,
        out_shape=jax.ShapeDtypeStruct((t, d), F32),
        grid=(t // tm,),
        in_specs=[pl.BlockSpec((tm, d), row),
                  pl.BlockSpec((1, 9, d), lambda i: (i // per_b, 0, 0)),
                  _const_spec((1, d))]
                 + [pl.BlockSpec((tm, a.shape[1]), row) for a in acts]
                 + [_const_spec(a.shape) for a in ws],
        out_specs=pl.BlockSpec((tm, d), row),
        compiler_params=_params(1),
    )(x, mod_l, g_post.reshape(1, d), *acts, *ws)


def kernel(x, c, mod_w, mod_b, norm_pre, norm_post, ffn_w_in, ffn_w_out, mix_w_in, forget_b,
           ssm_a_re, ssm_a_im, ssm_log_dt, ssm_b_re, ssm_b_im, ssm_c_re, ssm_c_im, ssm_d,
           glu_w, attn_w_out, mix_w_out):
    bsz, seq, d = x.shape
    depth = mod_w.shape[0]
    heads = forget_b.shape[1]
    dh = attn_w_out.shape[1] // heads
    groups = ssm_a_re.shape[1]
    ssm_w = ssm_d.shape[1]
    assert heads % 2 == 0 and 2 * dh == LANES and seq % SSM_CHUNK == 0
    tm = min(512, seq)
    tq = min(512, seq)
    assert seq % tm == 0 and seq % tq == 0

    mod = _mod_call(c, mod_w, mod_b).reshape(depth, bsz, 9, d)
    xt = x.reshape(bsz * seq, d)
    for l in range(depth):
        mod_l = mod[l]
        xt = _ffn_call(xt, mod_l, norm_pre[l, 0], norm_post[l, 0], ffn_w_in[l, 0], ffn_w_out[l, 0],
                       sub=0, seq=seq, tm=tm)
        u, qp, kp, v, f, ga, gb = _inproj_call(xt, mod_l, norm_pre[l, 1], mix_w_in[l], heads=heads,
                                               dh=dh, ssm_w=ssm_w, seq=seq, tm=tm)
        q_aug, k_aug = _attn_prep_call(qp, kp, f, forget_b[l], heads=heads, dh=dh, bsz=bsz,
                                       seq=seq, tm=tm)
        attn = _attn_call(q_aug, k_aug, v, heads=heads, dh=dh, bsz=bsz, seq=seq, tq=tq)
        mats = _ssm_prep_call(ssm_a_re[l], ssm_a_im[l], ssm_log_dt[l], ssm_b_re[l], ssm_b_im[l],
                              ssm_c_re[l], ssm_c_im[l], ssm_d[l])
        y_ssm = _ssm_call(u, mats, bsz=bsz)
        xt = _merge_call(xt, mod_l, norm_post[l, 1], y_ssm, attn, ga, gb, glu_w[l], attn_w_out[l],
                         mix_w_out[l], seq=seq, tm=tm)
        xt = _ffn_call(xt, mod_l, norm_pre[l, 2], norm_post[l, 2], ffn_w_in[l, 1], ffn_w_out[l, 1],
                       sub=2, seq=seq, tm=tm)
    return xt.reshape(bsz, seq, d)
```

```python
import functools
import math

import jax
import jax.numpy as jnp
from jax import lax
from jax.experimental import pallas as pl
from jax.experimental.pallas import tpu as pltpu

F32 = jnp.float32
BF16 = jnp.bfloat16

RMS_EPS = 1e-6
FFN_RES = 0.5
A_RE_MAX = -1e-4
LOG2E = math.log2(math.e)
LANES = 128
HEAD_PAD = 128
SSM_CHUNK = 16
SCAN_ROWS = 16
VMEM_LIMIT = 56 * 1024 * 1024
HIGHEST = lax.Precision.HIGHEST


def _dot(a, b, precision=None):
    return jnp.dot(a, b, preferred_element_type=F32, precision=precision)


def _rms(x, g):
    ms = jnp.mean(x * x, axis=-1, keepdims=True)
    return x * lax.rsqrt(ms + RMS_EPS) * g


def _silu(x):
    return x * jax.nn.sigmoid(x)


def _params(n_axes, semantics="arbitrary"):
    return pltpu.CompilerParams(dimension_semantics=(semantics,) * n_axes,
                                vmem_limit_bytes=VMEM_LIMIT)


def _const_spec(shape):
    zeros = (0,) * len(shape)
    return pl.BlockSpec(shape, lambda *_: zeros, pipeline_mode=pl.Buffered(1))


def _mod_kernel(c_ref, w_ref, b_ref, o_ref):
    sc = _silu(c_ref[...]).astype(BF16)
    o_ref[0] = _dot(sc, w_ref[0].astype(BF16)) + b_ref[0]


def _mod_call(c, mod_w, mod_b):
    depth, d, n = mod_w.shape
    bsz = c.shape[0]
    rows = 8
    c_pad = jnp.zeros((rows, d), F32).at[:bsz].set(c)
    tn = n // 4 if n % (4 * LANES) == 0 else n
    out = pl.pallas_call(
        _mod_kernel,
        out_shape=jax.ShapeDtypeStruct((depth, rows, n), F32),
        grid=(depth, n // tn),
        in_specs=[pl.BlockSpec((rows, d), lambda l, j: (0, 0)),
                  pl.BlockSpec((1, d, tn), lambda l, j: (l, 0, j)),
                  pl.BlockSpec((1, 1, tn), lambda l, j: (l, 0, j))],
        out_specs=pl.BlockSpec((1, rows, tn), lambda l, j: (l, 0, j)),
        compiler_params=_params(2),
    )(c_pad, mod_w, mod_b.reshape(depth, 1, n))
    return out[:, :bsz]


def _mod_rows(mod_ref, sub):
    shift = mod_ref[0, 3 * sub:3 * sub + 1, :]
    scale = mod_ref[0, 3 * sub + 1:3 * sub + 2, :]
    gate = mod_ref[0, 3 * sub + 2:3 * sub + 3, :]
    return shift, scale, gate


def _ffn_kernel(x_ref, mod_ref, gpre_ref, gpost_ref, wg_ref, wu_ref, wo_ref,
                o_ref, acc_ref, *, sub, tk):
    x = x_ref[...]
    shift, scale, gate = _mod_rows(mod_ref, sub)
    h = (_rms(x, gpre_ref[...]) * (1.0 + scale) + shift).astype(BF16)
    d_ff = wg_ref.shape[1]
    for k in range(d_ff // tk):
        sl = slice(k * tk, (k + 1) * tk)
        g = _dot(h, wg_ref[:, sl])
        u = _dot(h, wu_ref[:, sl])
        a = (_silu(g) * u).astype(BF16)
        contrib = _dot(a, wo_ref[sl, :])
        if k == 0:
            acc_ref[...] = contrib
        else:
            acc_ref[...] += contrib
    o_ref[...] = x + (FFN_RES * gate) * _rms(acc_ref[...], gpost_ref[...])


def _ffn_call(x, mod_l, g_pre, g_post, w_in, w_out, *, sub, seq, tm):
    t, d = x.shape
    d_ff = w_out.shape[0]
    tk = 256 if d_ff % 256 == 0 else d_ff
    w_in = w_in.astype(BF16)
    wg, wu = w_in[:, :d_ff], w_in[:, d_ff:]
    per_b = seq // tm
    row = lambda i: (i, 0)
    return pl.pallas_call(
        functools.partial(_ffn_kernel, sub=sub, tk=tk),
        out_shape=jax.ShapeDtypeStruct((t, d), F32),
        grid=(t // tm,),
        in_specs=[pl.BlockSpec((tm, d), row),
                  pl.BlockSpec((1, 9, d), lambda i: (i // per_b, 0, 0)),
                  _const_spec((1, d)), _const_spec((1, d)),
                  _const_spec((d, d_ff)), _const_spec((d, d_ff)), _const_spec((d_ff, d))],
        out_specs=pl.BlockSpec((tm, d), row),
        scratch_shapes=[pltpu.VMEM((tm, d), F32)],
        compiler_params=_params(1),
    )(x, mod_l, g_pre.reshape(1, d), g_post.reshape(1, d), wg, wu, w_out.astype(BF16))


def _inproj_kernel(x_ref, mod_ref, gpre_ref, wu_ref, wq_ref, wk_ref, wv_ref, wf_ref,
                   wga_ref, wgb_ref, u_ref, q_ref, k_ref, v_ref, f_ref, ga_ref, gb_ref):
    shift, scale, _ = _mod_rows(mod_ref, 1)
    h = (_rms(x_ref[...], gpre_ref[...]) * (1.0 + scale) + shift).astype(BF16)
    u_ref[...] = _dot(h, wu_ref[...]).astype(BF16)
    q_ref[...] = _dot(h, wq_ref[...]).astype(BF16)
    k_ref[...] = _dot(h, wk_ref[...]).astype(BF16)
    v_ref[...] = _dot(h, wv_ref[...]).astype(BF16)
    f_ref[...] = _dot(h, wf_ref[...])
    ga_ref[...] = jax.nn.sigmoid(_dot(h, wga_ref[...])).astype(BF16)
    gb_ref[...] = jax.nn.sigmoid(_dot(h, wgb_ref[...])).astype(BF16)


def _pad_heads(w, heads, dh, scale=1.0):
    d = w.shape[0]
    w = (w * scale).reshape(d, heads, dh)
    w = jnp.pad(w, ((0, 0), (0, 0), (0, HEAD_PAD - dh)))
    return w.reshape(d, heads * HEAD_PAD)


def _inproj_call(x, mod_l, g_pre, w_in, *, heads, dh, ssm_w, seq, tm):
    t, d = x.shape
    aw = heads * dh
    w = w_in
    o = 0
    wu = w[:, o:o + ssm_w]; o += ssm_w
    wq = w[:, o:o + aw]; o += aw
    wk = w[:, o:o + aw]; o += aw
    wv = w[:, o:o + aw]; o += aw
    wf = w[:, o:o + heads]; o += heads
    wga = w[:, o:o + d]; o += d
    wgb = w[:, o:o + d]
    wq = _pad_heads(wq, heads, dh, dh ** -0.5 * LOG2E)
    wk = _pad_heads(wk, heads, dh)
    wf = jnp.pad(wf, ((0, 0), (0, LANES - heads)))
    ws = [a.astype(BF16) for a in (wu, wq, wk, wv, wf, wga, wgb)]
    per_b = seq // tm
    row = lambda i: (i, 0)
    hp = heads * HEAD_PAD
    widths = (ssm_w, hp, hp, aw, LANES, d, d)
    dtypes = (BF16, BF16, BF16, BF16, F32, BF16, BF16)
    return pl.pallas_call(
        _inproj_kernel,
        out_shape=[jax.ShapeDtypeStruct((t, n), dt) for n, dt in zip(widths, dtypes)],
        grid=(t // tm,),
        in_specs=[pl.BlockSpec((tm, d), row),
                  pl.BlockSpec((1, 9, d), lambda i: (i // per_b, 0, 0)),
                  _const_spec((1, d))] + [_const_spec(a.shape) for a in ws],
        out_specs=[pl.BlockSpec((tm, n), row) for n in widths],
        compiler_params=_params(1),
    )(x, mod_l, g_pre.reshape(1, d), *ws)


def _split3(c):
    hi = c.astype(BF16).astype(F32)
    r = c - hi
    mid = r.astype(BF16).astype(F32)
    lo = (r - mid).astype(BF16).astype(F32)
    return hi, mid, lo


def _attn_prep_kernel(q_ref, k_ref, f_ref, fb_ref, qo_ref, ko_ref, carry_ref, *, heads, dh):
    tm = f_ref.shape[0]

    @pl.when(pl.program_id(1) == 0)
    def _():
        carry_ref[...] = jnp.zeros_like(carry_ref)

    log_f = jax.nn.log_sigmoid(f_ref[...] + fb_ref[...])
    r = lax.broadcasted_iota(jnp.int32, (tm, tm), 0)
    c = lax.broadcasted_iota(jnp.int32, (tm, tm), 1)
    tri = (c <= r).astype(F32)
    cum = _dot(tri, log_f, HIGHEST) + carry_ref[...]
    carry_ref[...] = cum[tm - 1:tm, :]

    lane = lax.broadcasted_iota(jnp.int32, (tm, HEAD_PAD), 1)
    for h in range(heads):
        hi, mid, lo = _split3(cum[:, h:h + 1] * LOG2E)
        one = jnp.ones((tm, 1), F32)
        zero = jnp.zeros((tm, HEAD_PAD), F32)

        def place(vals):
            out = zero
            for j, v in enumerate(vals):
                out = jnp.where(lane == dh + j, v, out)
            return out

        sl = slice(h * HEAD_PAD, (h + 1) * HEAD_PAD)
        q_aug = place((one, one, one, hi, mid, lo))
        k_aug = place((-hi, -mid, -lo, one, one, one))
        qo_ref[:, sl] = (q_ref[:, sl].astype(F32) + q_aug).astype(BF16)
        ko_ref[:, sl] = (k_ref[:, sl].astype(F32) + k_aug).astype(BF16)


def _attn_prep_call(qp, kp, f, forget_b, *, heads, dh, bsz, seq, tm):
    t, hp = qp.shape
    fb = jnp.pad(forget_b.reshape(1, heads), ((0, 0), (0, LANES - heads)))
    per_b = seq // tm
    row = lambda b, i: (b * per_b + i, 0)
    return pl.pallas_call(
        functools.partial(_attn_prep_kernel, heads=heads, dh=dh),
        out_shape=[jax.ShapeDtypeStruct((t, hp), BF16)] * 2,
        grid=(bsz, per_b),
        in_specs=[pl.BlockSpec((tm, hp), row), pl.BlockSpec((tm, hp), row),
                  pl.BlockSpec((tm, LANES), row),
                  pl.BlockSpec((1, LANES), lambda b, i: (0, 0))],
        out_specs=[pl.BlockSpec((tm, hp), row)] * 2,
        scratch_shapes=[pltpu.VMEM((1, LANES), F32)],
        compiler_params=_params(2),
    )(qp, kp, f, fb)


def _attn_kernel(q_ref, k_ref, vt_ref, o_ref, s_ref, *, tq, tk, dh):
    i = pl.program_id(2)
    assert tq == 2 * tk
    key =lax.broadcasted_iota(jnp.int32, (tk, tq), 0)
    qry = lax.broadcasted_iota(jnp.int32, (tk, tq), 1)
    head_cols = [slice(hh * HEAD_PAD, (hh + 1) * HEAD_PAD) for hh in range(2)]
    qs = [q_ref[:, hs] for hs in head_cols]

    def scores(j, slot, diag):
        rows = pl.ds(pl.multiple_of(j * tk, tk), tk)
        blk_max = []
        for hh in range(2):
            st = lax.dot_general(k_ref[rows, head_cols[hh]], qs[hh], (((1,), (1,)), ((), ())),
                                 preferred_element_type=F32)
            if diag is not None:
                st = jnp.where(key + diag * tk <= qry, st, -jnp.inf)
            s_ref[slot, hh] = st
            blk_max.append(jnp.max(st, axis=0, keepdims=True))
        return tuple(blk_max)

    def softmax_pv(j, slot, blk_max, state):
        out = []
        for hh in range(2):
            m, l, acc = state[hh]
            m_new = jnp.maximum(m, blk_max[hh])
            alpha = jnp.exp2(m - m_new)
            p = jnp.exp2(s_ref[slot, hh] - m_new)
            l = alpha * l + jnp.sum(p, axis=0, keepdims=True)
            acc = alpha * acc + _dot(vt_ref[j, hh * dh:(hh + 1) * dh, :], p.astype(BF16))
            out.append((m_new, l, acc))
        return tuple(out)

    init = (jnp.full((1, tq), -jnp.inf, F32), jnp.zeros((1, tq), F32), jnp.zeros((dh, tq), F32))
    d0, d1 = 2 * i, 2 * i + 1

    def body(t, carry):
        state, blk0, max0 = carry
        max1 = scores(2 * t, 1, None)
        state = softmax_pv(blk0, 0, max0, state)
        max0 = scores(2 * t + 1, 0, None)
        state = softmax_pv(2 * t, 1, max1, state)
        return state, 2 * t + 1, max0

    state, blk0, max0 = lax.fori_loop(0, i, body, ((init, init), d0, scores(d0, 0, 0)))
    max1 = scores(d1, 1, 1)
    state = softmax_pv(blk0, 0, max0, state)
    state = softmax_pv(d1, 1, max1, state)
    (_, l0, acc0), (_, l1, acc1) = state
    o_t = jnp.concatenate([acc0 / l0, acc1 / l1], axis=0)
    o_ref[...] = o_t.T.astype(BF16)


def _attn_call(q_aug, k_aug, v, *, heads, dh, bsz, seq, tq, tk):
    t = q_aug.shape[0]
    nq = seq // tq
    nkv = seq // tk
    pair = 2 * HEAD_PAD
    vt = v.reshape(bsz, nkv, tk, heads // 2, 2 * dh).transpose(0, 3, 1, 4, 2)
    return pl.pallas_call(
        functools.partial(_attn_kernel, tq=tq, tk=tk, dh=dh),
        out_shape=jax.ShapeDtypeStruct((t, heads * dh), BF16),
        grid=(bsz, heads // 2, nq),
        in_specs=[pl.BlockSpec((tq, pair), lambda b, h, i: (b * nq + i, h)),
                  pl.BlockSpec((seq, pair), lambda b, h, i: (b, h)),
                  pl.BlockSpec((None, None, nkv, 2 * dh, tk), lambda b, h, i: (b, h, 0, 0, 0))],
        out_specs=pl.BlockSpec((tq, 2 * dh), lambda b, h, i: (b * nq + i, h)),
        scratch_shapes=[pltpu.VMEM((2, 2, tk, tq), F32)],
        compiler_params=_params(3),
    )(q_aug, k_aug, vt)


def _ssm_prep_kernel(re_c_ref, im_c_ref, re_r_ref, im_r_ref, ldt_ref, ct_re_ref, ct_im_ref,
                     bt_re_ref, bt_im_ref, btt_re_ref, btt_im_ref, d_ref,
                     wi_ref, ws_ref, wo_ref, a1_ref, a2_ref, *, n_ch, p_st):
    cn = SSM_CHUNK * n_ch
    p2 = 2 * p_st
    shift_bits = n_ch.bit_length() - 1
    dt = jnp.exp(ldt_ref[0])

    are_c = jnp.minimum(re_c_ref[0], A_RE_MAX) * dt
    aim_c = im_c_ref[0] * dt
    re_r = jnp.minimum(re_r_ref[0], A_RE_MAX)
    im_r = im_r_ref[0]
    are_r = re_r * dt
    aim_r = im_r * dt

    mag = jnp.exp(are_r)
    lb_re = mag * jnp.cos(aim_r)
    lb_im = mag * jnp.sin(aim_r)
    den = re_r * re_r + im_r * im_r
    xr = lb_re - 1.0
    coef_re = (xr * re_r + lb_im * im_r) / den
    coef_im = (lb_im * re_r - xr * im_r) / den

    tau = lax.shift_right_logical(lax.broadcasted_iota(jnp.int32, (p2, cn), 1), shift_bits)
    top = lax.broadcasted_iota(jnp.int32, (p2, cn), 0) < p_st

    def lam_pow_c(off):
        e = (tau + off).astype(F32)
        pm = jnp.exp(e * are_c)
        pr = pm * jnp.cos(e * aim_c)
        pi = pm * jnp.sin(e * aim_c)
        rr = pr * ct_re_ref[0] - pi * ct_im_ref[0]
        ri = pr * ct_im_ref[0] + pi * ct_re_ref[0]
        return jnp.where(top, rr, -ri)

    wo_ref[0] = lam_pow_c(1).astype(BF16)

    lane_b = lax.broadcasted_iota(jnp.int32, (n_ch, p2), 1) < p_st
    bb_re = coef_re * bt_re_ref[0] - coef_im * bt_im_ref[0]
    bb_im = coef_re * bt_im_ref[0] + coef_im * bt_re_ref[0]
    z = _dot(jnp.where(lane_b, bb_re, bb_im), lam_pow_c(0), HIGHEST)
    zr = lax.broadcasted_iota(jnp.int32, (n_ch, cn), 0)
    zl = lax.broadcasted_iota(jnp.int32, (n_ch, cn), 1)
    z = z + jnp.where(zr == zl, d_ref[0], 0.0)
    for s in range(SSM_CHUNK):
        blk = z if s == 0 else jnp.where(zl >= s * n_ch, pltpu.roll(z, s * n_ch, axis=1), 0.0)
        wi_ref[0, s * n_ch:(s + 1) * n_ch, :] = blk.astype(BF16)

    srow = lax.shift_right_logical(lax.broadcasted_iota(jnp.int32, (cn, p2), 0), shift_bits)
    e = (SSM_CHUNK - 1 - srow).astype(F32)
    pm = jnp.exp(e * are_r)
    pr = pm * jnp.cos(e * aim_r)
    pi = pm * jnp.sin(e * aim_r)
    tb_re = coef_re * btt_re_ref[0] - coef_im * btt_im_ref[0]
    tb_im = coef_re * btt_im_ref[0] + coef_im * btt_re_ref[0]
    lane_s = lax.broadcasted_iota(jnp.int32, (cn, p2), 1) < p_st
    ws_ref[0] = jnp.where(lane_s, pr * tb_re - pi * tb_im, pr * tb_im + pi * tb_re).astype(BF16)

    ar, ai = lb_re, lb_im
    for _ in range(SSM_CHUNK.bit_length() - 1):
        ar, ai = ar * ar - ai * ai, 2.0 * ar * ai
    lane_a = lax.broadcasted_iota(jnp.int32, (1, p2), 1) < p_st
    for k in range(SCAN_ROWS):
        a1_ref[0, k:k + 1, :] = ar
        a2_ref[0, k:k + 1, :] = jnp.where(lane_a, -ai, ai)
        ar, ai = ar * ar - ai * ai, 2.0 * ar * ai


def _ssm_prep_call(a_re, a_im, log_dt, b_re, b_im, c_re, c_im, d_skip):
    g, p_st = a_re.shape
    n_ch = b_re.shape[-1]
    cn = SSM_CHUNK * n_ch
    p2 = 2 * p_st
    col = lambda a: jnp.tile(a[:, :, None], (1, 2, 1))
    rowv = lambda a: jnp.tile(a[:, None, :], (1, 1, 2))
    ct = lambda a: jnp.tile(a.transpose(0, 2, 1), (1, 2, SSM_CHUNK))
    bt = lambda a: jnp.tile(a.transpose(0, 2, 1), (1, 1, 2))
    btt = lambda a: jnp.tile(bt(a), (1, SSM_CHUNK, 1))
    d_pad = jnp.pad(d_skip.reshape(g, 1, n_ch), ((0, 0), (0, 0), (0, cn - n_ch)))
    args = (col(a_re), col(a_im), rowv(a_re), rowv(a_im), log_dt.reshape(g, 1, 1),
            ct(c_re), ct(c_im), bt(b_re), bt(b_im), btt(b_re), btt(b_im), d_pad)
    spec = lambda a: pl.BlockSpec((1,) + a.shape[1:], lambda i: (i, 0, 0))
    out_shapes = [jax.ShapeDtypeStruct((g, cn, cn), BF16),
                  jax.ShapeDtypeStruct((g, cn, p2), BF16),
                  jax.ShapeDtypeStruct((g, p2, cn), BF16),
                  jax.ShapeDtypeStruct((g, SCAN_ROWS, p2), F32),
                  jax.ShapeDtypeStruct((g, SCAN_ROWS, p2), F32)]
    return pl.pallas_call(
        functools.partial(_ssm_prep_kernel, n_ch=n_ch, p_st=p_st),
        out_shape=out_shapes,
        grid=(g,),
        in_specs=[spec(a) for a in args],
        out_specs=[spec(s) for s in out_shapes],
        compiler_params=_params(1),
    )(*args)


def _ssm_kernel(u_ref, wi_ref, ws_ref, wo_ref, a1_ref, a2_ref, y_ref, *, bsz, p_st):
    u = u_ref[0]
    nc = u.shape[0]
    ncb = nc // bsz
    s_all = _dot(u, ws_ref[0])
    row = lax.broadcasted_iota(jnp.int32, (ncb, 2 * p_st), 0)
    for b in range(bsz):
        rows = slice(b * ncb, (b + 1) * ncb)
        x = s_all[rows]
        d, k = 1, 0
        while d < ncb:
            xs = jnp.where(row >= d, pltpu.roll(x, d, axis=0), 0.0)
            x = x + a1_ref[0, k:k + 1, :] * xs + a2_ref[0, k:k + 1, :] * pltpu.roll(xs, p_st, axis=1)
            d, k = 2 * d, k + 1
        s_in = jnp.where(row >= 1, pltpu.roll(x, 1, axis=0), 0.0).astype(BF16)
        y_ref[0, rows, :] = _dot(u[rows], wi_ref[0]) + _dot(s_in, wo_ref[0])


def _ssm_call(u, mats, *, bsz):
    wi, ws, wo, a1, a2 = mats
    g, cn, _ = wi.shape
    p_st = ws.shape[2] // 2
    t, ssm_w = u.shape
    n_ch = ssm_w // g
    nc = t // SSM_CHUNK
    assert (nc // bsz) < 2 ** SCAN_ROWS
    ug = u.reshape(nc, SSM_CHUNK, g, n_ch).transpose(2, 0, 1, 3).reshape(g, nc, cn)
    spec = lambda a: pl.BlockSpec((1,) + a.shape[1:], lambda i: (i, 0, 0))
    y = pl.pallas_call(
        functools.partial(_ssm_kernel, bsz=bsz, p_st=p_st),
        out_shape=jax.ShapeDtypeStruct((g, nc, cn), F32),
        grid=(g,),
        in_specs=[spec(ug), spec(wi), spec(ws), spec(wo), spec(a1), spec(a2)],
        out_specs=pl.BlockSpec((1, nc, cn), lambda i: (i, 0, 0)),
        compiler_params=_params(1),
    )(ug, wi, ws, wo, a1, a2)
    return y.reshape(g, nc, SSM_CHUNK, n_ch).transpose(1, 2, 0, 3).reshape(t, ssm_w)


def _merge_kernel(x_ref, mod_ref, gpost_ref, ys_ref, at_ref, ga_ref, gb_ref,
                  wglu_ref, wat_ref, wout_ref, o_ref):
    d = x_ref.shape[1]
    _, _, gate = _mod_rows(mod_ref, 1)
    z = _dot(jax.nn.gelu(ys_ref[...]).astype(BF16), wglu_ref[...])
    y_a = z[:, :d] * jax.nn.sigmoid(z[:, d:])
    y_b = _dot(at_ref[...], wat_ref[...])
    merged = ga_ref[...].astype(F32) * y_a + gb_ref[...].astype(F32) * y_b
    y = _dot(merged.astype(BF16), wout_ref[...])
    o_ref[...] = x_ref[...] + gate * _rms(y, gpost_ref[...])


def _merge_call(x, mod_l, g_post, y_ssm, attn, ga, gb, glu_w, attn_w_out, w_out, *, seq, tm):
    t, d = x.shape
    per_b = seq // tm
    row = lambda i: (i, 0)
    ws = [glu_w.astype(BF16), attn_w_out.astype(BF16), w_out.astype(BF16)]
    acts = (y_ssm, attn, ga, gb)
    return pl.pallas_call(
        _merge_kernel,
        out_shape=jax.ShapeDtypeStruct((t, d), F32),
        grid=(t // tm,),
        in_specs=[pl.BlockSpec((tm, d), row),
                  pl.BlockSpec((1, 9, d), lambda i: (i // per_b, 0, 0)),
                  _const_spec((1, d))]
                 + [pl.BlockSpec((tm, a.shape[1]), row) for a in acts]
                 + [_const_spec(a.shape) for a in ws],
        out_specs=pl.BlockSpec((tm, d), row),
        compiler_params=_params(1),
    )(x, mod_l, g_post.reshape(1, d), *acts, *ws)


def kernel(x, c, mod_w, mod_b, norm_pre, norm_post, ffn_w_in, ffn_w_out, mix_w_in, forget_b,
           ssm_a_re, ssm_a_im, ssm_log_dt, ssm_b_re, ssm_b_im, ssm_c_re, ssm_c_im, ssm_d,
           glu_w, attn_w_out, mix_w_out):
    bsz, seq, d = x.shape
    depth = mod_w.shape[0]
    heads = forget_b.shape[1]
    dh = attn_w_out.shape[1] // heads
    groups = ssm_a_re.shape[1]
    ssm_w = ssm_d.shape[1]
    assert heads % 2 == 0 and 2 * dh == LANES and seq % SSM_CHUNK == 0
    tm = min(512, seq)
    tq = min(512, seq)
    tk = tq // 2
    assert seq % tm == 0 and seq % tq == 0

    mod = _mod_call(c, mod_w, mod_b).reshape(depth, bsz, 9, d)
    xt = x.reshape(bsz * seq, d)
    for l in range(depth):
        mod_l = mod[l]
        xt = _ffn_call(xt, mod_l, norm_pre[l, 0], norm_post[l, 0], ffn_w_in[l, 0], ffn_w_out[l, 0],
                       sub=0, seq=seq, tm=tm)
        u, qp, kp, v, f, ga, gb = _inproj_call(xt, mod_l, norm_pre[l, 1], mix_w_in[l], heads=heads,
                                               dh=dh, ssm_w=ssm_w, seq=seq, tm=tm)
        q_aug, k_aug = _attn_prep_call(qp, kp, f, forget_b[l], heads=heads, dh=dh, bsz=bsz,
                                       seq=seq, tm=tm)
        attn = _attn_call(q_aug, k_aug, v, heads=heads, dh=dh, bsz=bsz, seq=seq, tq=tq, tk=tk)
        mats = _ssm_prep_call(ssm_a_re[l], ssm_a_im[l], ssm_log_dt[l], ssm_b_re[l], ssm_b_im[l],
                              ssm_c_re[l], ssm_c_im[l], ssm_d[l])
        y_ssm = _ssm_call(u, mats, bsz=bsz)
        xt = _merge_call(xt, mod_l, norm_post[l, 1], y_ssm, attn, ga, gb, glu_w[l], attn_w_out[l],
                         mix_w_out[l], seq=seq, tm=tm)
        xt = _ffn_call(xt, mod_l, norm_pre[l, 2], norm_post[l, 2], ffn_w_in[l, 1], ffn_w_out[l, 1],
                       sub=2, seq=seq, tm=tm)
    return xt.reshape(bsz, seq, d)
```

```python
import functools
import math

import jax
import jax.numpy as jnp
from jax import lax
from jax.experimental import pallas as pl
from jax.experimental.pallas import tpu as pltpu

F32 = jnp.float32
BF16 = jnp.bfloat16

RMS_EPS = 1e-6
FFN_RES = 0.5
A_RE_MAX = -1e-4
LOG2E = math.log2(math.e)
LANES = 128
HEAD_PAD = 128
SSM_CHUNK = 16
SCAN_ROWS = 16
VMEM_LIMIT = 56 * 1024 * 1024
HIGHEST = lax.Precision.HIGHEST


def _dot(a, b, precision=None):
    return jnp.dot(a, b, preferred_element_type=F32, precision=precision)


def _rms(x, g):
    ms = jnp.mean(x * x, axis=-1, keepdims=True)
    return x * lax.rsqrt(ms + RMS_EPS) * g


def _silu(x):
    return x * jax.nn.sigmoid(x)


def _params(n_axes, semantics="arbitrary"):
    return pltpu.CompilerParams(dimension_semantics=(semantics,) * n_axes,
                                vmem_limit_bytes=VMEM_LIMIT)


def _const_spec(shape):
    zeros = (0,) * len(shape)
    return pl.BlockSpec(shape, lambda *_: zeros, pipeline_mode=pl.Buffered(1))


def _mod_kernel(c_ref, w_ref, b_ref, o_ref):
    sc = _silu(c_ref[...]).astype(BF16)
    o_ref[0] = _dot(sc, w_ref[0].astype(BF16)) + b_ref[0]


def _mod_call(c, mod_w, mod_b):
    depth, d, n = mod_w.shape
    bsz = c.shape[0]
    rows = 8
    c_pad = jnp.zeros((rows, d), F32).at[:bsz].set(c)
    tn = n // 4 if n % (4 * LANES) == 0 else n
    out = pl.pallas_call(
        _mod_kernel,
        out_shape=jax.ShapeDtypeStruct((depth, rows, n), F32),
        grid=(depth, n // tn),
        in_specs=[pl.BlockSpec((rows, d), lambda l, j: (0, 0)),
                  pl.BlockSpec((1, d, tn), lambda l, j: (l, 0, j)),
                  pl.BlockSpec((1, 1, tn), lambda l, j: (l, 0, j))],
        out_specs=pl.BlockSpec((1, rows, tn), lambda l, j: (l, 0, j)),
        compiler_params=_params(2),
    )(c_pad, mod_w, mod_b.reshape(depth, 1, n))
    return out[:, :bsz]


def _mod_rows(mod_ref, sub):
    shift = mod_ref[0, 3 * sub:3 * sub + 1, :]
    scale = mod_ref[0, 3 * sub + 1:3 * sub + 2, :]
    gate = mod_ref[0, 3 * sub + 2:3 * sub + 3, :]
    return shift, scale, gate


def _ffn_kernel(x_ref, mod_ref, gpre_ref, gpost_ref, wg_ref, wu_ref, wo_ref,
                o_ref, acc_ref, *, sub, tk):
    x = x_ref[...]
    shift, scale, gate = _mod_rows(mod_ref, sub)
    h = (_rms(x, gpre_ref[...]) * (1.0 + scale) + shift).astype(BF16)
    d_ff = wg_ref.shape[1]
    for k in range(d_ff // tk):
        sl = slice(k * tk, (k + 1) * tk)
        g = _dot(h, wg_ref[:, sl])
        u = _dot(h, wu_ref[:, sl])
        a = (_silu(g) * u).astype(BF16)
        contrib = _dot(a, wo_ref[sl, :])
        if k == 0:
            acc_ref[...] = contrib
        else:
            acc_ref[...] += contrib
    o_ref[...] = x + (FFN_RES * gate) * _rms(acc_ref[...], gpost_ref[...])


def _ffn_call(x, mod_l, g_pre, g_post, w_in, w_out, *, sub, seq, tm):
    t, d = x.shape
    d_ff = w_out.shape[0]
    tk = 256 if d_ff % 256 == 0 else d_ff
    w_in = w_in.astype(BF16)
    wg, wu = w_in[:, :d_ff], w_in[:, d_ff:]
    per_b = seq // tm
    row = lambda i: (i, 0)
    return pl.pallas_call(
        functools.partial(_ffn_kernel, sub=sub, tk=tk),
        out_shape=jax.ShapeDtypeStruct((t, d), F32),
        grid=(t // tm,),
        in_specs=[pl.BlockSpec((tm, d), row),
                  pl.BlockSpec((1, 9, d), lambda i: (i // per_b, 0, 0)),
                  _const_spec((1, d)), _const_spec((1, d)),
                  _const_spec((d, d_ff)), _const_spec((d, d_ff)), _const_spec((d_ff, d))],
        out_specs=pl.BlockSpec((tm, d), row),
        scratch_shapes=[pltpu.VMEM((tm, d), F32)],
        compiler_params=_params(1),
    )(x, mod_l, g_pre.reshape(1, d), g_post.reshape(1, d), wg, wu, w_out.astype(BF16))


def _split3(c):
    hi = c.astype(BF16).astype(F32)
    r = c - hi
    mid = r.astype(BF16).astype(F32)
    lo = (r - mid).astype(BF16).astype(F32)
    return hi, mid, lo


def _inproj_kernel(x_ref, mod_ref, gpre_ref, fb_ref, wu_ref, wq_ref, wk_ref, wv_ref, wf_ref,
                   wga_ref, wgb_ref, u_ref, q_ref, k_ref, vt_ref, ga_ref, gb_ref, carry_ref,
                   *, heads, dh, tk):
    tm = x_ref.shape[0]

    @pl.when(pl.program_id(1) == 0)
    def _():
        carry_ref[...] = jnp.zeros_like(carry_ref)

    shift, scale, _ = _mod_rows(mod_ref, 1)
    h = (_rms(x_ref[...], gpre_ref[...]) * (1.0 + scale) + shift).astype(BF16)
    u_ref[...] = _dot(h, wu_ref[...])
    ga_ref[...] = jax.nn.sigmoid(_dot(h, wga_ref[...])).astype(BF16)
    gb_ref[...] = jax.nn.sigmoid(_dot(h, wgb_ref[...])).astype(BF16)

    v_t = _dot(h, wv_ref[...]).T.astype(BF16)
    for hp in range(heads // 2):
        for blk in range(tm // tk):
            vt_ref[hp, blk] = v_t[hp * 2 * dh:(hp + 1) * 2 * dh, blk * tk:(blk + 1) * tk]

    log_f = jax.nn.log_sigmoid(_dot(h, wf_ref[...]) + fb_ref[...])
    r = lax.broadcasted_iota(jnp.int32, (tm, tm), 0)
    c = lax.broadcasted_iota(jnp.int32, (tm, tm), 1)
    cum = _dot((c <= r).astype(F32), log_f, HIGHEST) + carry_ref[...]
    carry_ref[...] = cum[tm - 1:tm, :]

    q = _dot(h, wq_ref[...])
    k = _dot(h, wk_ref[...])
    lane = lax.broadcasted_iota(jnp.int32, (tm, HEAD_PAD), 1)
    ones_q = jnp.where((lane >= dh) & (lane < dh + 3), 1.0, 0.0)
    ones_k = jnp.where((lane >= dh + 3) & (lane < dh + 6), 1.0, 0.0)
    for hd in range(heads):
        hi, mid, lo = _split3(cum[:, hd:hd + 1] * LOG2E)
        sl = slice(hd * HEAD_PAD, (hd + 1) * HEAD_PAD)
        q_aug = jnp.where(lane == dh + 3, hi, jnp.where(lane == dh + 4, mid,
                          jnp.where(lane == dh + 5, lo, ones_q)))
        k_aug = jnp.where(lane == dh, -hi, jnp.where(lane == dh + 1, -mid,
                          jnp.where(lane == dh + 2, -lo, ones_k)))
        q_ref[:, sl] = jnp.where(lane < dh, q[:, sl], q_aug).astype(BF16)
        k_ref[:, sl] = jnp.where(lane < dh, k[:, sl], k_aug).astype(BF16)


def _pad_heads(w, heads, dh, scale=1.0):
    d = w.shape[0]
    w = (w * scale).reshape(d, heads, dh)
    w = jnp.pad(w, ((0, 0), (0, 0), (0, HEAD_PAD - dh)))
    return w.reshape(d, heads * HEAD_PAD)


def _inproj_call(x, mod_l, g_pre, w_in, forget_b, *, heads, dh, ssm_w, bsz, seq, tm, tk):
    t, d = x.shape
    aw = heads * dh
    w = w_in
    o = 0
    wu = w[:, o:o + ssm_w]; o += ssm_w
    wq = w[:, o:o + aw]; o += aw
    wk = w[:, o:o + aw]; o += aw
    wv = w[:, o:o + aw]; o += aw
    wf = w[:, o:o + heads]; o += heads
    wga = w[:, o:o + d]; o += d
    wgb = w[:, o:o + d]
    wq = _pad_heads(wq, heads, dh, dh ** -0.5 * LOG2E)
    wk = _pad_heads(wk, heads, dh)
    wf = jnp.pad(wf, ((0, 0), (0, LANES - heads)))
    ws = [a.astype(BF16) for a in (wu, wq, wk, wv, wf, wga, wgb)]
    fb = jnp.pad(forget_b.reshape(1, heads), ((0, 0), (0, LANES - heads)))
    per_b = seq // tm
    row = lambda b, i: (b * per_b + i, 0)
    hp = heads * HEAD_PAD
    tok = lambda n, dt: (jax.ShapeDtypeStruct((t, n), dt), pl.BlockSpec((tm, n), row))
    vt = (jax.ShapeDtypeStruct((bsz, heads // 2, seq // tk, 2 * dh, tk), BF16),
          pl.BlockSpec((None, heads // 2, tm // tk, 2 * dh, tk), lambda b, i: (b, 0, i, 0, 0)))
    outs = [tok(ssm_w, F32), tok(hp, BF16), tok(hp, BF16), vt, tok(d, BF16), tok(d, BF16)]
    return pl.pallas_call(
        functools.partial(_inproj_kernel, heads=heads, dh=dh, tk=tk),
        out_shape=[o[0] for o in outs],
        grid=(bsz, per_b),
        in_specs=[pl.BlockSpec((tm, d), row),
                  pl.BlockSpec((1, 9, d), lambda b, i: (b, 0, 0)),
                  _const_spec((1, d)), _const_spec((1, LANES))] + [_const_spec(a.shape) for a in ws],
        out_specs=[o[1] for o in outs],
        scratch_shapes=[pltpu.VMEM((1, LANES), F32)],
        compiler_params=_params(2),
    )(x, mod_l, g_pre.reshape(1, d), fb, *ws)


def _attn_kernel(q_ref, k_ref, vt_ref, o_ref, s_ref, *, tq, tk, dh):
    i = pl.program_id(2)
    assert tq == 2 * tk
    key =lax.broadcasted_iota(jnp.int32, (tk, tq), 0)
    qry = lax.broadcasted_iota(jnp.int32, (tk, tq), 1)
    head_cols = [slice(hh * HEAD_PAD, (hh + 1) * HEAD_PAD) for hh in range(2)]
    qs = [q_ref[:, hs] for hs in head_cols]

    def scores(j, slot, diag):
        rows = pl.ds(pl.multiple_of(j * tk, tk), tk)
        blk_max = []
        for hh in range(2):
            st = lax.dot_general(k_ref[rows, head_cols[hh]], qs[hh], (((1,), (1,)), ((), ())),
                                 preferred_element_type=F32)
            if diag is not None:
                st = jnp.where(key + diag * tk <= qry, st, -jnp.inf)
            s_ref[slot, hh] = st
            blk_max.append(jnp.max(st, axis=0, keepdims=True))
        return tuple(blk_max)

    def softmax_pv(j, slot, blk_max, state):
        out = []
        for hh in range(2):
            m, l, acc = state[hh]
            m_new = jnp.maximum(m, blk_max[hh])
            alpha = jnp.exp2(m - m_new)
            p = jnp.exp2(s_ref[slot, hh] - m_new)
            l = alpha * l + jnp.sum(p, axis=0, keepdims=True)
            acc = alpha * acc + _dot(vt_ref[j, hh * dh:(hh + 1) * dh, :], p.astype(BF16))
            out.append((m_new, l, acc))
        return tuple(out)

    init = (jnp.full((1, tq), -jnp.inf, F32), jnp.zeros((1, tq), F32), jnp.zeros((dh, tq), F32))
    d0, d1 = 2 * i, 2 * i + 1

    def body(t, carry):
        state, blk0, max0 = carry
        max1 = scores(2 * t, 1, None)
        state = softmax_pv(blk0, 0, max0, state)
        max0 = scores(2 * t + 1, 0, None)
        state = softmax_pv(2 * t, 1, max1, state)
        return state, 2 * t + 1, max0

    state, blk0, max0 = lax.fori_loop(0, i, body, ((init, init), d0, scores(d0, 0, 0)))
    max1 = scores(d1, 1, 1)
    state = softmax_pv(blk0, 0, max0, state)
    state = softmax_pv(d1, 1, max1, state)
    (_, l0, acc0), (_, l1, acc1) = state
    o_t = jnp.concatenate([acc0 / l0, acc1 / l1], axis=0)
    o_ref[...] = o_t.T.astype(BF16)


def _attn_call(q_aug, k_aug, vt, *, heads, dh, bsz, seq, tq, tk):
    t = q_aug.shape[0]
    nq = seq // tq
    nkv = seq // tk
    pair = 2 * HEAD_PAD
    return pl.pallas_call(
        functools.partial(_attn_kernel, tq=tq, tk=tk, dh=dh),
        out_shape=jax.ShapeDtypeStruct((t, heads * dh), BF16),
        grid=(bsz, heads // 2, nq),
        in_specs=[pl.BlockSpec((tq, pair), lambda b, h, i: (b * nq + i, h)),
                  pl.BlockSpec((seq, pair), lambda b, h, i: (b, h)),
                  pl.BlockSpec((None, None, nkv, 2 * dh, tk), lambda b, h, i: (b, h, 0, 0, 0))],
        out_specs=pl.BlockSpec((tq, 2 * dh), lambda b, h, i: (b * nq + i, h)),
        scratch_shapes=[pltpu.VMEM((2, 2, tk, tq), F32)],
        compiler_params=_params(3),
    )(q_aug, k_aug, vt)


def _ssm_prep_kernel(re_c_ref, im_c_ref, re_r_ref, im_r_ref, ldt_ref, ct_re_ref, ct_im_ref,
                     bt_re_ref, bt_im_ref, btt_re_ref, btt_im_ref, d_ref,
                     wi_ref, ws_ref, wo_ref, a1_ref, a2_ref, *, n_ch, p_st):
    cn = SSM_CHUNK * n_ch
    p2 = 2 * p_st
    shift_bits = n_ch.bit_length() - 1
    dt = jnp.exp(ldt_ref[0])

    are_c = jnp.minimum(re_c_ref[0], A_RE_MAX) * dt
    aim_c = im_c_ref[0] * dt
    re_r = jnp.minimum(re_r_ref[0], A_RE_MAX)
    im_r = im_r_ref[0]
    are_r = re_r * dt
    aim_r = im_r * dt

    mag = jnp.exp(are_r)
    lb_re = mag * jnp.cos(aim_r)
    lb_im = mag * jnp.sin(aim_r)
    den = re_r * re_r + im_r * im_r
    xr = lb_re - 1.0
    coef_re = (xr * re_r + lb_im * im_r) / den
    coef_im = (lb_im * re_r - xr * im_r) / den

    tau = lax.shift_right_logical(lax.broadcasted_iota(jnp.int32, (p2, cn), 1), shift_bits)
    top = lax.broadcasted_iota(jnp.int32, (p2, cn), 0) < p_st

    def lam_pow_c(off):
        e = (tau + off).astype(F32)
        pm = jnp.exp(e * are_c)
        pr = pm * jnp.cos(e * aim_c)
        pi = pm * jnp.sin(e * aim_c)
        rr = pr * ct_re_ref[0] - pi * ct_im_ref[0]
        ri = pr * ct_im_ref[0] + pi * ct_re_ref[0]
        return jnp.where(top, rr, -ri)

    wo_ref[0] = lam_pow_c(1).astype(BF16)

    lane_b = lax.broadcasted_iota(jnp.int32, (n_ch, p2), 1) < p_st
    bb_re = coef_re * bt_re_ref[0] - coef_im * bt_im_ref[0]
    bb_im = coef_re * bt_im_ref[0] + coef_im * bt_re_ref[0]
    z = _dot(jnp.where(lane_b, bb_re, bb_im), lam_pow_c(0), HIGHEST)
    zr = lax.broadcasted_iota(jnp.int32, (n_ch, cn), 0)
    zl = lax.broadcasted_iota(jnp.int32, (n_ch, cn), 1)
    z = z + jnp.where(zr == zl, d_ref[0], 0.0)
    for s in range(SSM_CHUNK):
        blk = z if s == 0 else jnp.where(zl >= s * n_ch, pltpu.roll(z, s * n_ch, axis=1), 0.0)
        wi_ref[0, s * n_ch:(s + 1) * n_ch, :] = blk.astype(BF16)

    srow = lax.shift_right_logical(lax.broadcasted_iota(jnp.int32, (cn, p2), 0), shift_bits)
    e = (SSM_CHUNK - 1 - srow).astype(F32)
    pm = jnp.exp(e * are_r)
    pr = pm * jnp.cos(e * aim_r)
    pi = pm * jnp.sin(e * aim_r)
    tb_re = coef_re * btt_re_ref[0] - coef_im * btt_im_ref[0]
    tb_im = coef_re * btt_im_ref[0] + coef_im * btt_re_ref[0]
    lane_s = lax.broadcasted_iota(jnp.int32, (cn, p2), 1) < p_st
    ws_ref[0] = jnp.where(lane_s, pr * tb_re - pi * tb_im, pr * tb_im + pi * tb_re).astype(BF16)

    ar, ai = lb_re, lb_im
    for _ in range(SSM_CHUNK.bit_length() - 1):
        ar, ai = ar * ar - ai * ai, 2.0 * ar * ai
    lane_a = lax.broadcasted_iota(jnp.int32, (1, p2), 1) < p_st
    for k in range(SCAN_ROWS):
        a1_ref[0, k:k + 1, :] = ar
        a2_ref[0, k:k + 1, :] = jnp.where(lane_a, -ai, ai)
        ar, ai = ar * ar - ai * ai, 2.0 * ar * ai


def _ssm_prep_call(a_re, a_im, log_dt, b_re, b_im, c_re, c_im, d_skip):
    g, p_st = a_re.shape
    n_ch = b_re.shape[-1]
    cn = SSM_CHUNK * n_ch
    p2 = 2 * p_st
    col = lambda a: jnp.tile(a[:, :, None], (1, 2, 1))
    rowv = lambda a: jnp.tile(a[:, None, :], (1, 1, 2))
    ct = lambda a: jnp.tile(a.transpose(0, 2, 1), (1, 2, SSM_CHUNK))
    bt = lambda a: jnp.tile(a.transpose(0, 2, 1), (1, 1, 2))
    btt = lambda a: jnp.tile(bt(a), (1, SSM_CHUNK, 1))
    d_pad = jnp.pad(d_skip.reshape(g, 1, n_ch), ((0, 0), (0, 0), (0, cn - n_ch)))
    args = (col(a_re), col(a_im), rowv(a_re), rowv(a_im), log_dt.reshape(g, 1, 1),
            ct(c_re), ct(c_im), bt(b_re), bt(b_im), btt(b_re), btt(b_im), d_pad)
    spec = lambda a: pl.BlockSpec((1,) + a.shape[1:], lambda i: (i, 0, 0))
    out_shapes = [jax.ShapeDtypeStruct((g, cn, cn), BF16),
                  jax.ShapeDtypeStruct((g, cn, p2), BF16),
                  jax.ShapeDtypeStruct((g, p2, cn), BF16),
                  jax.ShapeDtypeStruct((g, SCAN_ROWS, p2), F32),
                  jax.ShapeDtypeStruct((g, SCAN_ROWS, p2), F32)]
    return pl.pallas_call(
        functools.partial(_ssm_prep_kernel, n_ch=n_ch, p_st=p_st),
        out_shape=out_shapes,
        grid=(g,),
        in_specs=[spec(a) for a in args],
        out_specs=[spec(s) for s in out_shapes],
        compiler_params=_params(1),
    )(*args)


def _ssm_kernel(u_ref, wi_ref, ws_ref, wo_ref, a1_ref, a2_ref, y_ref, *, bsz, p_st):
    u = u_ref[0]
    nc = u.shape[0]
    ncb = nc // bsz
    s_all = _dot(u, ws_ref[0])
    row = lax.broadcasted_iota(jnp.int32, (ncb, 2 * p_st), 0)
    for b in range(bsz):
        rows = slice(b * ncb, (b + 1) * ncb)
        x = s_all[rows]
        d, k = 1, 0
        while d < ncb:
            xs = jnp.where(row >= d, pltpu.roll(x, d, axis=0), 0.0)
            x = x + a1_ref[0, k:k + 1, :] * xs + a2_ref[0, k:k + 1, :] * pltpu.roll(xs, p_st, axis=1)
            d, k = 2 * d, k + 1
        s_in = jnp.where(row >= 1, pltpu.roll(x, 1, axis=0), 0.0).astype(BF16)
        y_ref[0, rows, :] = _dot(u[rows], wi_ref[0]) + _dot(s_in, wo_ref[0])


def _ssm_call(u, mats, *, bsz):
    wi, ws, wo, a1, a2 = mats
    g, cn, _ = wi.shape
    p_st = ws.shape[2] // 2
    t, ssm_w = u.shape
    n_ch = ssm_w // g
    nc = t // SSM_CHUNK
    assert (nc // bsz) < 2 ** SCAN_ROWS
    ug = u.reshape(nc, SSM_CHUNK, g, n_ch).transpose(2, 0, 1, 3).reshape(g, nc, cn)
    spec = lambda a: pl.BlockSpec((1,) + a.shape[1:], lambda i: (i, 0, 0))
    y = pl.pallas_call(
        functools.partial(_ssm_kernel, bsz=bsz, p_st=p_st),
        out_shape=jax.ShapeDtypeStruct((g, nc, cn), F32),
        grid=(g,),
        in_specs=[spec(ug), spec(wi), spec(ws), spec(wo), spec(a1), spec(a2)],
        out_specs=pl.BlockSpec((1, nc, cn), lambda i: (i, 0, 0)),
        compiler_params=_params(1),
    )(ug, wi, ws, wo, a1, a2)
    return y.reshape(g, nc, SSM_CHUNK, n_ch).transpose(1, 2, 0, 3).reshape(t, ssm_w)


def _s5_prep_kernel(re_c_ref, im_c_ref, re_r_ref, im_r_ref, ldt_ref, ct_re_ref, ct_im_ref,
                    bt_re_ref, bt_im_ref, btt_re_ref, btt_im_ref, d_ref,
                    mst_ref, ws_ref, wo_ref, a1_ref, a2_ref, b_scr, *, n_ch, p_st):
    cn = SSM_CHUNK * n_ch
    p2 = 2 * p_st
    gpb = LANES // n_ch
    assert cn == 2 * LANES and p2 == LANES
    shift_bits = n_ch.bit_length() - 1
    tau = lax.shift_right_logical(lax.broadcasted_iota(jnp.int32, (p2, cn), 1), shift_bits)
    top = lax.broadcasted_iota(jnp.int32, (p2, cn), 0) < p_st
    lane_b = lax.broadcasted_iota(jnp.int32, (n_ch, p2), 1) < p_st
    zr = lax.broadcasted_iota(jnp.int32, (n_ch, cn), 0)
    zl = lax.broadcasted_iota(jnp.int32, (n_ch, cn), 1)
    srow = lax.shift_right_logical(lax.broadcasted_iota(jnp.int32, (cn, p2), 0), shift_bits)
    lane_s = lax.broadcasted_iota(jnp.int32, (cn, p2), 1) < p_st
    lane_a = lax.broadcasted_iota(jnp.int32, (1, p2), 1) < p_st
    grp_z = lax.shift_right_logical(lax.broadcasted_iota(jnp.int32, (n_ch, LANES), 1), shift_bits)
    grp_r = lax.shift_right_logical(lax.broadcasted_iota(jnp.int32, (p2, LANES), 1), shift_bits)

    ws_ref[0] = jnp.zeros(ws_ref.shape[1:], BF16)
    for g in range(gpb):
        dt = jnp.exp(ldt_ref[g])
        are_c = jnp.minimum(re_c_ref[g], A_RE_MAX) * dt
        aim_c = im_c_ref[g] * dt
        re_r = jnp.minimum(re_r_ref[g], A_RE_MAX)
        im_r = im_r_ref[g]
        are_r = re_r * dt
        aim_r = im_r * dt

        mag = jnp.exp(are_r)
        lb_re = mag * jnp.cos(aim_r)
        lb_im = mag * jnp.sin(aim_r)
        den = re_r * re_r + im_r * im_r
        xr = lb_re - 1.0
        coef_re = (xr * re_r + lb_im * im_r) / den
        coef_im = (lb_im * re_r - xr * im_r) / den

        def lam_pow_c(off):
            e = (tau + off).astype(F32)
            pm = jnp.exp(e * are_c)
            pr = pm * jnp.cos(e * aim_c)
            pi = pm * jnp.sin(e * aim_c)
            rr = pr * ct_re_ref[g] - pi * ct_im_ref[g]
            ri = pr * ct_im_ref[g] + pi * ct_re_ref[g]
            return jnp.where(top, rr, -ri)

        r1 = lam_pow_c(1)
        bb_re = coef_re * bt_re_ref[g] - coef_im * bt_im_ref[g]
        bb_im = coef_re * bt_im_ref[g] + coef_im * bt_re_ref[g]
        z = _dot(jnp.where(lane_b, bb_re, bb_im), lam_pow_c(0), HIGHEST)
        z = z + jnp.where(zr == zl, d_ref[g], 0.0)

        e = (SSM_CHUNK - 1 - srow).astype(F32)
        pm = jnp.exp(e * are_r)
        pr = pm * jnp.cos(e * aim_r)
        pi = pm * jnp.sin(e * aim_r)
        tb_re = coef_re * btt_re_ref[g] - coef_im * btt_im_ref[g]
        tb_im = coef_re * btt_im_ref[g] + coef_im * btt_re_ref[g]
        ws_g = jnp.where(lane_s, pr * tb_re - pi * tb_im, pr * tb_im + pi * tb_re).astype(BF16)

        for t in range(SSM_CHUNK):
            half = slice((t // gpb) * LANES, (t // gpb + 1) * LANES)
            shift = ((g - t % gpb) * n_ch) % LANES
            move = (lambda a: a) if shift == 0 else (lambda a: pltpu.roll(a, shift, axis=1))
            b_scr[t, g * n_ch:(g + 1) * n_ch, :] = jnp.where(grp_z == g, move(z[:, half]), 0.0)
            wo_ref[0, g * p2:(g + 1) * p2, t * LANES:(t + 1) * LANES] = (
                jnp.where(grp_r == g, move(r1[:, half]), 0.0).astype(BF16))
            ws_ref[0, t * LANES + g * n_ch:t * LANES + (g + 1) * n_ch, g * p2:(g + 1) * p2] = (
                ws_g[t * n_ch:(t + 1) * n_ch, :])

        ar, ai = lb_re, lb_im
        for _ in range(SSM_CHUNK.bit_length() - 1):
            ar, ai = ar * ar - ai * ai, 2.0 * ar * ai
        for k in range(SCAN_ROWS):
            a1_ref[0, k:k + 1, g * p2:(g + 1) * p2] = ar
            a2_ref[0, k:k + 1, g * p2:(g + 1) * p2] = jnp.where(lane_a, -ai, ai)
            ar, ai = ar * ar - ai * ai, 2.0 * ar * ai

    pairs = SSM_CHUNK // 2
    for d in range(pairs):
        r0 = (pairs - 1 - d) * 2 * LANES
        diag = b_scr[2 * d].astype(BF16)
        below = b_scr[2 * d - 1].astype(BF16) if d > 0 else jnp.zeros((LANES, LANES), BF16)
        mst_ref[0, r0:r0 + LANES, :LANES] = diag
        mst_ref[0, r0:r0 + LANES, LANES:] = b_scr[2 * d + 1].astype(BF16)
        mst_ref[0, r0 + LANES:r0 + 2 * LANES, :LANES] = below
        mst_ref[0, r0 + LANES:r0 + 2 * LANES, LANES:] = diag


def _s5_prep_call(a_re, a_im, log_dt, b_re, b_im, c_re, c_im, d_skip):
    g, p_st = a_re.shape
    n_ch = b_re.shape[-1]
    cn = SSM_CHUNK * n_ch
    p2 = 2 * p_st
    gpb = LANES // n_ch
    nblk = g // gpb
    assert g % gpb == 0
    col = lambda a: jnp.tile(a[:, :, None], (1, 2, 1))
    rowv = lambda a: jnp.tile(a[:, None, :], (1, 1, 2))
    ct = lambda a: jnp.tile(a.transpose(0, 2, 1), (1, 2, SSM_CHUNK))
    bt = lambda a: jnp.tile(a.transpose(0, 2, 1), (1, 1, 2))
    btt = lambda a: jnp.tile(bt(a), (1, SSM_CHUNK, 1))
    d_pad = jnp.pad(d_skip.reshape(g, 1, n_ch), ((0, 0), (0, 0), (0, cn - n_ch)))
    args = (col(a_re), col(a_im), rowv(a_re), rowv(a_im), log_dt.reshape(g, 1, 1),
            ct(c_re), ct(c_im), bt(b_re), bt(b_im), btt(b_re), btt(b_im), d_pad)
    out_shapes = [jax.ShapeDtypeStruct((nblk, SSM_CHUNK * LANES, 2 * LANES), BF16),
                  jax.ShapeDtypeStruct((nblk, SSM_CHUNK * LANES, gpb * p2), BF16),
                  jax.ShapeDtypeStruct((nblk, gpb * p2, SSM_CHUNK * LANES), BF16),
                  jax.ShapeDtypeStruct((nblk, SCAN_ROWS, gpb * p2), F32),
                  jax.ShapeDtypeStruct((nblk, SCAN_ROWS, gpb * p2), F32)]
    return pl.pallas_call(
        functools.partial(_s5_prep_kernel, n_ch=n_ch, p_st=p_st),
        out_shape=out_shapes,
        grid=(nblk,),
        in_specs=[pl.BlockSpec((gpb,) + a.shape[1:], lambda i: (i, 0, 0)) for a in args],
        out_specs=[pl.BlockSpec((1,) + s.shape[1:], lambda i: (i, 0, 0)) for s in out_shapes],
        scratch_shapes=[pltpu.VMEM((SSM_CHUNK, LANES, LANES), F32)],
        compiler_params=_params(1),
    )(*args)


def _s5_kernel(u_ref, mst_ref, ws_ref, wo_ref, a1_ref, a2_ref, y_ref, xx_ref, sin_ref, *, p_st):
    ncb = xx_ref.shape[0]
    p2 = 2 * p_st
    pair = 2 * LANES
    pairs = SSM_CHUNK // 2
    for t in range(SSM_CHUNK):
        xx_ref[:, t * LANES:(t + 1) * LANES] = u_ref[pl.ds(t, ncb, stride=SSM_CHUNK), :].astype(BF16)
    s_all = _dot(xx_ref[...], ws_ref[0])
    row = lax.broadcasted_iota(jnp.int32, (ncb, p2), 0)
    for g in range(s_all.shape[1] // p2):
        cols = slice(g * p2, (g + 1) * p2)
        x = s_all[:, cols]
        d, k = 1, 0
        while d < ncb:
            xs = jnp.where(row >= d, pltpu.roll(x, d, axis=0), 0.0)
            x = x + a1_ref[0, k:k + 1, cols] * xs + a2_ref[0, k:k + 1, cols] * pltpu.roll(xs, p_st, axis=1)
            d, k = 2 * d, k + 1
        sin_ref[:, cols] = jnp.where(row >= 1, pltpu.roll(x, 1, axis=0), 0.0).astype(BF16)
    for tp in range(pairs):
        yy = (_dot(xx_ref[:, :pair * (tp + 1)], mst_ref[0, (pairs - 1 - tp) * pair:, :])
              + _dot(sin_ref[...], wo_ref[0, :, tp * pair:(tp + 1) * pair]))
        y_ref[pl.ds(2 * tp, ncb, stride=SSM_CHUNK), :] = yy[:, :LANES]
        y_ref[pl.ds(2 * tp + 1, ncb, stride=SSM_CHUNK), :] = yy[:, LANES:]


def _s5_call(u, mats, *, bsz, seq):
    mst, ws, wo, a1, a2 = mats
    nblk = mst.shape[0]
    t, ssm_w = u.shape
    ncb = seq // SSM_CHUNK
    p_st = LANES // 2
    assert ssm_w == nblk * LANES and ncb < 2 ** SCAN_ROWS
    wspec = lambda a: pl.BlockSpec((1,) + a.shape[1:], lambda o, b: (o, 0, 0))
    return pl.pallas_call(
        functools.partial(_s5_kernel, p_st=p_st),
        out_shape=jax.ShapeDtypeStruct((t, ssm_w), F32),
        grid=(nblk, bsz),
        in_specs=[pl.BlockSpec((seq, LANES), lambda o, b: (b, o))] + [wspec(a) for a in mats],
        out_specs=pl.BlockSpec((seq, LANES), lambda o, b: (b, o)),
        scratch_shapes=[pltpu.VMEM((ncb, SSM_CHUNK * LANES), BF16),
                        pltpu.VMEM((ncb, ws.shape[2]), BF16)],
        compiler_params=_params(2),
    )(u, *mats)


def _merge_kernel(x_ref, mod_ref, gpost_ref, ys_ref, at_ref, ga_ref, gb_ref,
                  wglu_ref, wat_ref, wout_ref, o_ref):
    d = x_ref.shape[1]
    _, _, gate = _mod_rows(mod_ref, 1)
    z = _dot(jax.nn.gelu(ys_ref[...]).astype(BF16), wglu_ref[...])
    y_a = z[:, :d] * jax.nn.sigmoid(z[:, d:])
    y_b = _dot(at_ref[...], wat_ref[...])
    merged = ga_ref[...].astype(F32) * y_a + gb_ref[...].astype(F32) * y_b
    y = _dot(merged.astype(BF16), wout_ref[...])
    o_ref[...] = x_ref[...] + gate * _rms(y, gpost_ref[...])


def _merge_call(x, mod_l, g_post, y_ssm, attn, ga, gb, glu_w, attn_w_out, w_out, *, seq, tm):
    t, d = x.shape
    per_b = seq // tm
    row = lambda i: (i, 0)
    ws = [glu_w.astype(BF16), attn_w_out.astype(BF16), w_out.astype(BF16)]
    acts = (y_ssm, attn, ga, gb)
    return pl.pallas_call(
        _merge_kernel,
        out_shape=jax.ShapeDtypeStruct((t, d), F32),
        grid=(t // tm,),
        in_specs=[pl.BlockSpec((tm, d), row),
                  pl.BlockSpec((1, 9, d), lambda i: (i // per_b, 0, 0)),
                  _const_spec((1, d))]
                 + [pl.BlockSpec((tm, a.shape[1]), row) for a in acts]
                 + [_const_spec(a.shape) for a in ws],
        out_specs=pl.BlockSpec((tm, d), row),
        compiler_params=_params(1),
    )(x, mod_l, g_post.reshape(1, d), *acts, *ws)


def kernel(x, c, mod_w, mod_b, norm_pre, norm_post, ffn_w_in, ffn_w_out, mix_w_in, forget_b,
           ssm_a_re, ssm_a_im, ssm_log_dt, ssm_b_re, ssm_b_im, ssm_c_re, ssm_c_im, ssm_d,
           glu_w, attn_w_out, mix_w_out):
    bsz, seq, d = x.shape
    depth = mod_w.shape[0]
    heads = forget_b.shape[1]
    dh = attn_w_out.shape[1] // heads
    groups = ssm_a_re.shape[1]
    ssm_w = ssm_d.shape[1]
    assert heads % 2 == 0 and 2 * dh == LANES and seq % SSM_CHUNK == 0
    tm = min(512, seq)
    tq = min(512, seq)
    tk = tq // 2
    assert seq % tm == 0 and seq % tq == 0

    mod = _mod_call(c, mod_w, mod_b).reshape(depth, bsz, 9, d)
    xt = x.reshape(bsz * seq, d)
    for l in range(depth):
        mod_l = mod[l]
        xt = _ffn_call(xt, mod_l, norm_pre[l, 0], norm_post[l, 0], ffn_w_in[l, 0], ffn_w_out[l, 0],
                       sub=0, seq=seq, tm=tm)
        u, q_aug, k_aug, vt, ga, gb = _inproj_call(xt, mod_l, norm_pre[l, 1], mix_w_in[l], forget_b[l],
                                                   heads=heads, dh=dh, ssm_w=ssm_w, bsz=bsz, seq=seq,
                                                   tm=tm, tk=tk)
        attn = _attn_call(q_aug, k_aug, vt, heads=heads, dh=dh, bsz=bsz, seq=seq, tq=tq, tk=tk)
        mats = _s5_prep_call(ssm_a_re[l], ssm_a_im[l], ssm_log_dt[l], ssm_b_re[l], ssm_b_im[l],
                             ssm_c_re[l], ssm_c_im[l], ssm_d[l])
        y_ssm = _s5_call(u, mats, bsz=bsz, seq=seq)
        xt = _merge_call(xt, mod_l, norm_post[l, 1], y_ssm, attn, ga, gb, glu_w[l], attn_w_out[l],
                         mix_w_out[l], seq=seq, tm=tm)
        xt = _ffn_call(xt, mod_l, norm_pre[l, 2], norm_post[l, 2], ffn_w_in[l, 1], ffn_w_out[l, 1],
                       sub=2, seq=seq, tm=tm)
    return xt.reshape(bsz, seq, d)
```

```python
import functools
import math

import jax
import jax.numpy as jnp
from jax import lax
from jax.experimental import pallas as pl
from jax.experimental.pallas import tpu as pltpu

F32 = jnp.float32
BF16 = jnp.bfloat16

RMS_EPS = 1e-6
FFN_RES = 0.5
A_RE_MAX = -1e-4
LOG2E = math.log2(math.e)
LANES = 128
HEAD_PAD = 128
V_EXTRA = 16
SSM_CHUNK = 16
SCAN_ROWS = 16
VMEM_LIMIT = 56 * 1024 * 1024
HIGHEST = lax.Precision.HIGHEST


def _dot(a, b, precision=None):
    return jnp.dot(a, b, preferred_element_type=F32, precision=precision)


def _rms(x, g):
    ms = jnp.mean(x * x, axis=-1, keepdims=True)
    return x * lax.rsqrt(ms + RMS_EPS) * g


def _silu(x):
    return x * jax.nn.sigmoid(x)


def _params(n_axes, semantics="arbitrary"):
    return pltpu.CompilerParams(dimension_semantics=(semantics,) * n_axes,
                                vmem_limit_bytes=VMEM_LIMIT)


def _const_spec(shape):
    zeros = (0,) * len(shape)
    return pl.BlockSpec(shape, lambda *_: zeros, pipeline_mode=pl.Buffered(1))


def _mod_kernel(c_ref, w_ref, b_ref, o_ref):
    sc = _silu(c_ref[...]).astype(BF16)
    o_ref[0] = _dot(sc, w_ref[0].astype(BF16)) + b_ref[0]


def _mod_call(c, mod_w, mod_b):
    depth, d, n = mod_w.shape
    bsz = c.shape[0]
    rows = 8
    c_pad = jnp.zeros((rows, d), F32).at[:bsz].set(c)
    tn = n // 4 if n % (4 * LANES) == 0 else n
    out = pl.pallas_call(
        _mod_kernel,
        out_shape=jax.ShapeDtypeStruct((depth, rows, n), F32),
        grid=(depth, n // tn),
        in_specs=[pl.BlockSpec((rows, d), lambda l, j: (0, 0)),
                  pl.BlockSpec((1, d, tn), lambda l, j: (l, 0, j)),
                  pl.BlockSpec((1, 1, tn), lambda l, j: (l, 0, j))],
        out_specs=pl.BlockSpec((1, rows, tn), lambda l, j: (l, 0, j)),
        compiler_params=_params(2),
    )(c_pad, mod_w, mod_b.reshape(depth, 1, n))
    return out[:, :bsz]


def _mod_rows(mod_ref, sub):
    shift = mod_ref[0, 3 * sub:3 * sub + 1, :]
    scale = mod_ref[0, 3 * sub + 1:3 * sub + 2, :]
    gate = mod_ref[0, 3 * sub + 2:3 * sub + 3, :]
    return shift, scale, gate


def _ffn_kernel(x_ref, mod_ref, gpre_ref, gpost_ref, wg_ref, wu_ref, wo_ref,
                o_ref, acc_ref, *, sub, tk):
    x = x_ref[...]
    shift, scale, gate = _mod_rows(mod_ref, sub)
    h = (_rms(x, gpre_ref[...]) * (1.0 + scale) + shift).astype(BF16)
    d_ff = wg_ref.shape[1]
    for k in range(d_ff // tk):
        sl = slice(k * tk, (k + 1) * tk)
        g = _dot(h, wg_ref[:, sl])
        u = _dot(h, wu_ref[:, sl])
        a = (_silu(g) * u).astype(BF16)
        contrib = _dot(a, wo_ref[sl, :])
        if k == 0:
            acc_ref[...] = contrib
        else:
            acc_ref[...] += contrib
    o_ref[...] = x + (FFN_RES * gate) * _rms(acc_ref[...], gpost_ref[...])


def _ffn_call(x, mod_l, g_pre, g_post, w_in, w_out, *, sub, seq, tm):
    t, d = x.shape
    d_ff = w_out.shape[0]
    tk = 256 if d_ff % 256 == 0 else d_ff
    w_in = w_in.astype(BF16)
    wg, wu = w_in[:, :d_ff], w_in[:, d_ff:]
    per_b = seq // tm
    row = lambda i: (i, 0)
    return pl.pallas_call(
        functools.partial(_ffn_kernel, sub=sub, tk=tk),
        out_shape=jax.ShapeDtypeStruct((t, d), F32),
        grid=(t // tm,),
        in_specs=[pl.BlockSpec((tm, d), row),
                  pl.BlockSpec((1, 9, d), lambda i: (i // per_b, 0, 0)),
                  _const_spec((1, d)), _const_spec((1, d)),
                  _const_spec((d, d_ff)), _const_spec((d, d_ff)), _const_spec((d_ff, d))],
        out_specs=pl.BlockSpec((tm, d), row),
        scratch_shapes=[pltpu.VMEM((tm, d), F32)],
        compiler_params=_params(1),
    )(x, mod_l, g_pre.reshape(1, d), g_post.reshape(1, d), wg, wu, w_out.astype(BF16))


def _split3(c):
    hi = c.astype(BF16).astype(F32)
    r = c - hi
    mid = r.astype(BF16).astype(F32)
    lo = (r - mid).astype(BF16).astype(F32)
    return hi, mid, lo


def _inproj_kernel(x_ref, mod_ref, gpre_ref, fb_ref, wu_ref, wq_ref, wk_ref, wv_ref, wf_ref,
                   wga_ref, wgb_ref, u_ref, q_ref, k_ref, vt_ref, ga_ref, gb_ref, carry_ref,
                   *, heads, dh, tk):
    tm = x_ref.shape[0]

    @pl.when(pl.program_id(1) == 0)
    def _():
        carry_ref[...] = jnp.zeros_like(carry_ref)

    shift, scale, _ = _mod_rows(mod_ref, 1)
    h = (_rms(x_ref[...], gpre_ref[...]) * (1.0 + scale) + shift).astype(BF16)
    u_ref[...] = _dot(h, wu_ref[...])
    ga_ref[...] = jax.nn.sigmoid(_dot(h, wga_ref[...])).astype(BF16)
    gb_ref[...] = jax.nn.sigmoid(_dot(h, wgb_ref[...])).astype(BF16)

    v_t = _dot(h, wv_ref[...]).T.astype(BF16)
    ones_rows = (lax.broadcasted_iota(jnp.int32, (V_EXTRA, tk), 0) == 0).astype(BF16)
    for hd in range(heads):
        r0 = (hd % 2) * (dh + V_EXTRA)
        for blk in range(tm // tk):
            vt_ref[hd // 2, blk, r0:r0 + dh, :] = v_t[hd * dh:(hd + 1) * dh, blk * tk:(blk + 1) * tk]
            vt_ref[hd // 2, blk, r0 + dh:r0 + dh + V_EXTRA, :] = ones_rows

    lane = lax.broadcasted_iota(jnp.int32, (tm, LANES), 1)
    log_f = jax.nn.log_sigmoid(_dot(h, wf_ref[...]) + fb_ref[...])
    hi, mid, lo = _split3(jnp.where(lane < heads, log_f, 0.0))
    packed = hi + pltpu.roll(mid, heads, axis=1) + pltpu.roll(lo, 2 * heads, axis=1)
    r = lax.broadcasted_iota(jnp.int32, (tm, tm), 0)
    c = lax.broadcasted_iota(jnp.int32, (tm, tm), 1)
    part = _dot((c <= r).astype(BF16), packed.astype(BF16))
    part = part + pltpu.roll(part, LANES - heads, axis=1) + pltpu.roll(part, LANES - 2 * heads, axis=1)
    cum = jnp.where(lane < heads, part, 0.0) + carry_ref[...]
    carry_ref[...] = cum[tm - 1:tm, :]

    q = _dot(h, wq_ref[...])
    k = _dot(h, wk_ref[...])
    ones_q = jnp.where((lane >= dh) & (lane < dh + 3), 1.0, 0.0)
    ones_k = jnp.where((lane >= dh + 3) & (lane < dh + 6), 1.0, 0.0)
    for hd in range(heads):
        hi, mid, lo = _split3(cum[:, hd:hd + 1] * LOG2E)
        q_aug = jnp.where(lane == dh + 3, hi, jnp.where(lane == dh + 4, mid,
                          jnp.where(lane == dh + 5, lo, ones_q)))
        k_aug = jnp.where(lane == dh, -hi, jnp.where(lane == dh + 1, -mid,
                          jnp.where(lane == dh + 2, -lo, ones_k)))
        src = slice((hd // 2) * LANES, (hd // 2 + 1) * LANES)
        move = (lambda a: a) if hd % 2 == 0 else (lambda a: pltpu.roll(a, dh, axis=1))
        dst = slice(hd * HEAD_PAD, (hd + 1) * HEAD_PAD)
        q_ref[:, dst] = jnp.where(lane < dh, move(q[:, src]), q_aug).astype(BF16)
        k_ref[:, dst] = jnp.where(lane < dh, move(k[:, src]), k_aug).astype(BF16)


def _inproj_call(x, mod_l, g_pre, w_in, forget_b, *, heads, dh, ssm_w, bsz, seq, tm, tk):
    t, d = x.shape
    aw = heads * dh
    w = w_in
    o = 0
    wu = w[:, o:o + ssm_w]; o += ssm_w
    wq = w[:, o:o + aw]; o += aw
    wk = w[:, o:o + aw]; o += aw
    wv = w[:, o:o + aw]; o += aw
    wf = w[:, o:o + heads]; o += heads
    wga = w[:, o:o + d]; o += d
    wgb = w[:, o:o + d]
    wq = wq * (dh ** -0.5 * LOG2E)
    wf = jnp.pad(wf, ((0, 0), (0, LANES - heads)))
    ws = [a.astype(BF16) for a in (wu, wq, wk, wv, wf, wga, wgb)]
    fb = jnp.pad(forget_b.reshape(1, heads), ((0, 0), (0, LANES - heads)))
    per_b = seq // tm
    row = lambda b, i: (b * per_b + i, 0)
    hp = heads * HEAD_PAD
    tok = lambda n, dt: (jax.ShapeDtypeStruct((t, n), dt), pl.BlockSpec((tm, n), row))
    vrows = 2 * (dh + V_EXTRA)
    vt = (jax.ShapeDtypeStruct((bsz, heads // 2, seq // tk, vrows, tk), BF16),
          pl.BlockSpec((None, heads // 2, tm // tk, vrows, tk), lambda b, i: (b, 0, i, 0, 0)))
    outs = [tok(ssm_w, F32), tok(hp, BF16), tok(hp, BF16), vt, tok(d, BF16), tok(d, BF16)]
    return pl.pallas_call(
        functools.partial(_inproj_kernel, heads=heads, dh=dh, tk=tk),
        out_shape=[o[0] for o in outs],
        grid=(bsz, per_b),
        in_specs=[pl.BlockSpec((tm, d), row),
                  pl.BlockSpec((1, 9, d), lambda b, i: (b, 0, 0)),
                  _const_spec((1, d)), _const_spec((1, LANES))] + [_const_spec(a.shape) for a in ws],
        out_specs=[o[1] for o in outs],
        scratch_shapes=[pltpu.VMEM((1, LANES), F32)],
        compiler_params=_params(2),
    )(x, mod_l, g_pre.reshape(1, d), fb, *ws)


def _attn_kernel(q_ref, k_ref, vt_ref, o_ref, s_ref, *, tq, tk, dh):
    i = pl.program_id(2)
    assert tq == 2 * tk
    key =lax.broadcasted_iota(jnp.int32, (tk, tq), 0)
    qry = lax.broadcasted_iota(jnp.int32, (tk, tq), 1)
    head_cols = [slice(hh * HEAD_PAD, (hh + 1) * HEAD_PAD) for hh in range(2)]
    qs = [q_ref[:, hs] for hs in head_cols]

    def scores(j, slot, diag):
        rows = pl.ds(pl.multiple_of(j * tk, tk), tk)
        blk_max = []
        for hh in range(2):
            st = lax.dot_general(k_ref[rows, head_cols[hh]], qs[hh], (((1,), (1,)), ((), ())),
                                 preferred_element_type=F32)
            if diag is not None:
                st = jnp.where(key + diag * tk <= qry, st, -jnp.inf)
            s_ref[slot, hh] = st
            blk_max.append(jnp.max(st, axis=0, keepdims=True))
        return tuple(blk_max)

    vrows = dh + V_EXTRA

    def softmax_pv(j, slot, blk_max, state):
        out = []
        for hh in range(2):
            m, acc = state[hh]
            m_new = jnp.maximum(m, blk_max[hh])
            alpha = jnp.exp2(m - m_new)
            p = jnp.exp2(s_ref[slot, hh] - m_new)
            acc = alpha * acc + _dot(vt_ref[j, hh * vrows:(hh + 1) * vrows, :], p.astype(BF16))
            out.append((m_new, acc))
        return tuple(out)

    init = (jnp.full((1, tq), -jnp.inf, F32), jnp.zeros((vrows, tq), F32))
    d0, d1 = 2 * i, 2 * i + 1

    def body(t, carry):
        state, blk0, max0 = carry
        max1 = scores(2 * t, 1, None)
        state = softmax_pv(blk0, 0, max0, state)
        max0 = scores(2 * t + 1, 0, None)
        state = softmax_pv(2 * t, 1, max1, state)
        return state, 2 * t + 1, max0

    state, blk0, max0 = lax.fori_loop(0, i, body, ((init, init), d0, scores(d0, 0, 0)))
    max1 = scores(d1, 1, 1)
    state = softmax_pv(blk0, 0, max0, state)
    state = softmax_pv(d1, 1, max1, state)
    o_t = jnp.concatenate([acc[:dh] / acc[dh:dh + 1] for _, acc in state], axis=0)
    o_ref[...] = o_t.T.astype(BF16)


def _attn_call(q_aug, k_aug, vt, *, heads, dh, bsz, seq, tq, tk):
    t = q_aug.shape[0]
    nq = seq // tq
    nkv = seq // tk
    pair = 2 * HEAD_PAD
    return pl.pallas_call(
        functools.partial(_attn_kernel, tq=tq, tk=tk, dh=dh),
        out_shape=jax.ShapeDtypeStruct((t, heads * dh), BF16),
        grid=(bsz, heads // 2, nq),
        in_specs=[pl.BlockSpec((tq, pair), lambda b, h, i: (b * nq + i, h)),
                  pl.BlockSpec((seq, pair), lambda b, h, i: (b, h)),
                  pl.BlockSpec((None, None, nkv, vt.shape[3], tk), lambda b, h, i: (b, h, 0, 0, 0))],
        out_specs=pl.BlockSpec((tq, 2 * dh), lambda b, h, i: (b * nq + i, h)),
        scratch_shapes=[pltpu.VMEM((2, 2, tk, tq), F32)],
        compiler_params=_params(3),
    )(q_aug, k_aug, vt)


def _s5_prep_kernel(re_c_ref, im_c_ref, re_r_ref, im_r_ref, ldt_ref, ct_re_ref, ct_im_ref,
                    bt_re_ref, bt_im_ref, d_ref,
                    mst_ref, ws_ref, wo_ref, a1_ref, a2_ref, b_scr, *, n_ch, p_st):
    cn = SSM_CHUNK * n_ch
    p2 = 2 * p_st
    gpb = LANES // n_ch
    assert cn == 2 * LANES and p2 == LANES
    shift_bits = n_ch.bit_length() - 1
    tau = lax.shift_right_logical(lax.broadcasted_iota(jnp.int32, (p2, cn), 1), shift_bits)
    top = lax.broadcasted_iota(jnp.int32, (p2, cn), 0) < p_st
    lane_b = lax.broadcasted_iota(jnp.int32, (n_ch, p2), 1) < p_st
    zr = lax.broadcasted_iota(jnp.int32, (n_ch, cn), 0)
    zl = lax.broadcasted_iota(jnp.int32, (n_ch, cn), 1)
    srow = lax.shift_right_logical(lax.broadcasted_iota(jnp.int32, (cn, p2), 0), shift_bits)
    lane_s = lax.broadcasted_iota(jnp.int32, (cn, p2), 1) < p_st
    lane_a = lax.broadcasted_iota(jnp.int32, (1, p2), 1) < p_st
    grp_z = lax.shift_right_logical(lax.broadcasted_iota(jnp.int32, (n_ch, LANES), 1), shift_bits)
    grp_r = lax.shift_right_logical(lax.broadcasted_iota(jnp.int32, (p2, LANES), 1), shift_bits)
    spread = (lax.broadcasted_iota(jnp.int32, (n_ch, cn), 0)
              == (lax.broadcasted_iota(jnp.int32, (n_ch, cn), 1) & (n_ch - 1))).astype(F32)

    ws_ref[0] = jnp.zeros(ws_ref.shape[1:], BF16)
    for g in range(gpb):
        dt = jnp.exp(ldt_ref[g])
        are_c = jnp.minimum(re_c_ref[g], A_RE_MAX) * dt
        aim_c = im_c_ref[g] * dt
        re_r = jnp.minimum(re_r_ref[g], A_RE_MAX)
        im_r = im_r_ref[g]
        are_r = re_r * dt
        aim_r = im_r * dt

        mag = jnp.exp(are_r)
        lb_re = mag * jnp.cos(aim_r)
        lb_im = mag * jnp.sin(aim_r)
        den = re_r * re_r + im_r * im_r
        xr = lb_re - 1.0
        coef_re = (xr * re_r + lb_im * im_r) / den
        coef_im = (lb_im * re_r - xr * im_r) / den

        ct_re = _dot(ct_re_ref[g], spread, HIGHEST)
        ct_im = _dot(ct_im_ref[g], spread, HIGHEST)

        def lam_pow_c(off):
            e = (tau + off).astype(F32)
            pm = jnp.exp(e * are_c)
            pr = pm * jnp.cos(e * aim_c)
            pi = pm * jnp.sin(e * aim_c)
            rr = pr * ct_re - pi * ct_im
            ri = pr * ct_im + pi * ct_re
            return jnp.where(top, rr, -ri)

        r1 = lam_pow_c(1)
        bb_re = coef_re * bt_re_ref[g] - coef_im * bt_im_ref[g]
        bb_im = coef_re * bt_im_ref[g] + coef_im * bt_re_ref[g]
        z = _dot(jnp.where(lane_b, bb_re, bb_im), lam_pow_c(0), HIGHEST)
        z = z + jnp.where(zr == zl, d_ref[g], 0.0)

        e = (SSM_CHUNK - 1 - srow).astype(F32)
        pm = jnp.exp(e * are_r)
        pr = pm * jnp.cos(e * aim_r)
        pi = pm * jnp.sin(e * aim_r)
        tb_re = jnp.concatenate([bb_re] * SSM_CHUNK, axis=0)
        tb_im = jnp.concatenate([bb_im] * SSM_CHUNK, axis=0)
        ws_g =jnp.where(lane_s, pr * tb_re - pi * tb_im, pr * tb_im + pi * tb_re).astype(BF16)

        for t in range(SSM_CHUNK):
            half = slice((t // gpb) * LANES, (t // gpb + 1) * LANES)
            shift = ((g - t % gpb) * n_ch) % LANES
            move = (lambda a: a) if shift == 0 else (lambda a: pltpu.roll(a, shift, axis=1))
            b_scr[t, g * n_ch:(g + 1) * n_ch, :] = jnp.where(grp_z == g, move(z[:, half]), 0.0)
            wo_ref[0, g * p2:(g + 1) * p2, t * LANES:(t + 1) * LANES] = (
                jnp.where(grp_r == g, move(r1[:, half]), 0.0).astype(BF16))
            ws_ref[0, t * LANES + g * n_ch:t * LANES + (g + 1) * n_ch, g * p2:(g + 1) * p2] = (
                ws_g[t * n_ch:(t + 1) * n_ch, :])

        ar, ai = lb_re, lb_im
        for _ in range(SSM_CHUNK.bit_length() - 1):
            ar, ai = ar * ar - ai * ai, 2.0 * ar * ai
        for k in range(SCAN_ROWS):
            a1_ref[0, k:k + 1, g * p2:(g + 1) * p2] = ar
            a2_ref[0, k:k + 1, g * p2:(g + 1) * p2] = jnp.where(lane_a, -ai, ai)
            ar, ai = ar * ar - ai * ai, 2.0 * ar * ai

    pairs = SSM_CHUNK // 2
    for d in range(pairs):
        r0 = (pairs - 1 - d) * 2 * LANES
        diag = b_scr[2 * d].astype(BF16)
        below = b_scr[2 * d - 1].astype(BF16) if d > 0 else jnp.zeros((LANES, LANES), BF16)
        mst_ref[0, r0:r0 + LANES, :LANES] = diag
        mst_ref[0, r0:r0 + LANES, LANES:] = b_scr[2 * d + 1].astype(BF16)
        mst_ref[0, r0 + LANES:r0 + 2 * LANES, :LANES] = below
        mst_ref[0, r0 + LANES:r0 + 2 * LANES, LANES:] = diag


def _s5_prep_call(a_re, a_im, log_dt, b_re, b_im, c_re, c_im, d_skip):
    p_st, n_ch = a_re.shape[-1], b_re.shape[-1]
    flat = lambda a, k: a.reshape((-1,) + a.shape[a.ndim - k:])
    a_re, a_im, log_dt = flat(a_re, 1), flat(a_im, 1), flat(log_dt, 0)
    b_re, b_im, c_re, c_im = flat(b_re, 2), flat(b_im, 2), flat(c_re, 2), flat(c_im, 2)
    g = a_re.shape[0]
    cn = SSM_CHUNK * n_ch
    p2 = 2 * p_st
    gpb = LANES // n_ch
    nblk = g // gpb
    assert g % gpb == 0
    col = lambda a: jnp.tile(a[:, :, None], (1, 2, 1))
    rowv = lambda a: jnp.tile(a[:, None, :], (1, 1, 2))
    ct = lambda a: jnp.tile(a.transpose(0, 2, 1), (1, 2, 1))
    bt = lambda a: jnp.tile(a.transpose(0, 2, 1), (1, 1, 2))
    d_pad = jnp.pad(d_skip.reshape(g, 1, n_ch), ((0, 0), (0, 0), (0, cn - n_ch)))
    args = (col(a_re), col(a_im), rowv(a_re), rowv(a_im), log_dt.reshape(g, 1, 1),
            ct(c_re), ct(c_im), bt(b_re), bt(b_im), d_pad)
    out_shapes = [jax.ShapeDtypeStruct((nblk, SSM_CHUNK * LANES, 2 * LANES), BF16),
                  jax.ShapeDtypeStruct((nblk, SSM_CHUNK * LANES, gpb * p2), BF16),
                  jax.ShapeDtypeStruct((nblk, gpb * p2, SSM_CHUNK * LANES), BF16),
                  jax.ShapeDtypeStruct((nblk, SCAN_ROWS, gpb * p2), F32),
                  jax.ShapeDtypeStruct((nblk, SCAN_ROWS, gpb * p2), F32)]
    return pl.pallas_call(
        functools.partial(_s5_prep_kernel, n_ch=n_ch, p_st=p_st),
        out_shape=out_shapes,
        grid=(nblk,),
        in_specs=[pl.BlockSpec((gpb,) + a.shape[1:], lambda i: (i, 0, 0)) for a in args],
        out_specs=[pl.BlockSpec((1,) + s.shape[1:], lambda i: (i, 0, 0)) for s in out_shapes],
        scratch_shapes=[pltpu.VMEM((SSM_CHUNK, LANES, LANES), F32)],
        compiler_params=_params(1),
    )(*args)


def _s5_kernel(u_ref, mst_ref, ws_ref, wo_ref, a1_ref, a2_ref, y_ref, xx_ref, sin_ref, *, p_st):
    ncb = xx_ref.shape[0]
    p2 = 2 * p_st
    pair = 2 * LANES
    pairs = SSM_CHUNK // 2
    for t in range(SSM_CHUNK):
        xx_ref[:, t * LANES:(t + 1) * LANES] = u_ref[pl.ds(t, ncb, stride=SSM_CHUNK), :].astype(BF16)
    s_all = _dot(xx_ref[...], ws_ref[0])
    row = lax.broadcasted_iota(jnp.int32, (ncb, p2), 0)
    for g in range(s_all.shape[1] // p2):
        cols = slice(g * p2, (g + 1) * p2)
        x = s_all[:, cols]
        d, k = 1, 0
        while d < ncb:
            xs = jnp.where(row >= d, pltpu.roll(x, d, axis=0), 0.0)
            x = x + a1_ref[0, k:k + 1, cols] * xs + a2_ref[0, k:k + 1, cols] * pltpu.roll(xs, p_st, axis=1)
            d, k = 2 * d, k + 1
        sin_ref[:, cols] = jnp.where(row >= 1, pltpu.roll(x, 1, axis=0), 0.0).astype(BF16)
    for tp in range(pairs):
        yy = (_dot(xx_ref[:, :pair * (tp + 1)], mst_ref[0, (pairs - 1 - tp) * pair:, :])
              + _dot(sin_ref[...], wo_ref[0, :, tp * pair:(tp + 1) * pair]))
        y_ref[pl.ds(2 * tp, ncb, stride=SSM_CHUNK), :] = yy[:, :LANES]
        y_ref[pl.ds(2 * tp + 1, ncb, stride=SSM_CHUNK), :] = yy[:, LANES:]


def _s5_call(u, mats, layer, *, bsz, seq):
    mst, ws, wo, a1, a2 = mats
    t, ssm_w = u.shape
    nblk = ssm_w // LANES
    ncb = seq // SSM_CHUNK
    p_st = LANES // 2
    assert ncb < 2 ** SCAN_ROWS
    wspec = lambda a: pl.BlockSpec((1,) + a.shape[1:], lambda o, b: (layer * nblk + o, 0, 0))
    return pl.pallas_call(
        functools.partial(_s5_kernel, p_st=p_st),
        out_shape=jax.ShapeDtypeStruct((t, ssm_w), F32),
        grid=(nblk, bsz),
        in_specs=[pl.BlockSpec((seq, LANES), lambda o, b: (b, o))] + [wspec(a) for a in mats],
        out_specs=pl.BlockSpec((seq, LANES), lambda o, b: (b, o)),
        scratch_shapes=[pltpu.VMEM((ncb, SSM_CHUNK * LANES), BF16),
                        pltpu.VMEM((ncb, ws.shape[2]), BF16)],
        compiler_params=_params(2),
    )(u, *mats)


def _merge_kernel(x_ref, mod_ref, gpost_ref, ys_ref, at_ref, ga_ref, gb_ref,
                  wglu_ref, wat_ref, wout_ref, o_ref):
    d = x_ref.shape[1]
    _, _, gate = _mod_rows(mod_ref, 1)
    z = _dot(jax.nn.gelu(ys_ref[...]).astype(BF16), wglu_ref[...])
    y_a = z[:, :d] * jax.nn.sigmoid(z[:, d:])
    y_b = _dot(at_ref[...], wat_ref[...])
    merged = ga_ref[...].astype(F32) * y_a + gb_ref[...].astype(F32) * y_b
    y = _dot(merged.astype(BF16), wout_ref[...])
    o_ref[...] = x_ref[...] + gate * _rms(y, gpost_ref[...])


def _merge_call(x, mod_l, g_post, y_ssm, attn, ga, gb, glu_w, attn_w_out, w_out, *, seq, tm):
    t, d = x.shape
    per_b = seq // tm
    row = lambda i: (i, 0)
    ws = [glu_w.astype(BF16), attn_w_out.astype(BF16), w_out.astype(BF16)]
    acts = (y_ssm, attn, ga, gb)
    return pl.pallas_call(
        _merge_kernel,
        out_shape=jax.ShapeDtypeStruct((t, d), F32),
        grid=(t // tm,),
        in_specs=[pl.BlockSpec((tm, d), row),
                  pl.BlockSpec((1, 9, d), lambda i: (i // per_b, 0, 0)),
                  _const_spec((1, d))]
                 + [pl.BlockSpec((tm, a.shape[1]), row) for a in acts]
                 + [_const_spec(a.shape) for a in ws],
        out_specs=pl.BlockSpec((tm, d), row),
        compiler_params=_params(1),
    )(x, mod_l, g_post.reshape(1, d), *acts, *ws)


def kernel(x, c, mod_w, mod_b, norm_pre, norm_post, ffn_w_in, ffn_w_out, mix_w_in, forget_b,
           ssm_a_re, ssm_a_im, ssm_log_dt, ssm_b_re, ssm_b_im, ssm_c_re, ssm_c_im, ssm_d,
           glu_w, attn_w_out, mix_w_out):
    bsz, seq, d = x.shape
    depth = mod_w.shape[0]
    heads = forget_b.shape[1]
    dh = attn_w_out.shape[1] // heads
    groups = ssm_a_re.shape[1]
    ssm_w = ssm_d.shape[1]
    assert heads % 2 == 0 and 2 * dh == LANES and seq % SSM_CHUNK == 0
    tm = min(512, seq)
    tq = min(512, seq)
    tk = tq // 2
    assert seq % tm == 0 and seq % tq == 0

    mod = _mod_call(c, mod_w, mod_b).reshape(depth, bsz, 9, d)
    mats = _s5_prep_call(ssm_a_re, ssm_a_im, ssm_log_dt, ssm_b_re, ssm_b_im, ssm_c_re, ssm_c_im, ssm_d)
    xt = x.reshape(bsz * seq, d)
    for l in range(depth):
        mod_l = mod[l]
        xt = _ffn_call(xt, mod_l, norm_pre[l, 0], norm_post[l, 0], ffn_w_in[l, 0], ffn_w_out[l, 0],
                       sub=0, seq=seq, tm=tm)
        u, q_aug, k_aug, vt, ga, gb = _inproj_call(xt, mod_l, norm_pre[l, 1], mix_w_in[l], forget_b[l],
                                                   heads=heads, dh=dh, ssm_w=ssm_w, bsz=bsz, seq=seq,
                                                   tm=tm, tk=tk)
        attn = _attn_call(q_aug, k_aug, vt, heads=heads, dh=dh, bsz=bsz, seq=seq, tq=tq, tk=tk)
        y_ssm = _s5_call(u, mats, l, bsz=bsz, seq=seq)
        xt = _merge_call(xt, mod_l, norm_post[l, 1], y_ssm, attn, ga, gb, glu_w[l], attn_w_out[l],
                         mix_w_out[l], seq=seq, tm=tm)
        xt = _ffn_call(xt, mod_l, norm_pre[l, 2], norm_post[l, 2], ffn_w_in[l, 1], ffn_w_out[l, 1],
                       sub=2, seq=seq, tm=tm)
    return xt.reshape(bsz, seq, d)
```

```python
import functools
import math

import jax
import jax.numpy as jnp
from jax import lax
from jax.experimental import pallas as pl
from jax.experimental.pallas import tpu as pltpu

F32 = jnp.float32
BF16 = jnp.bfloat16

RMS_EPS = 1e-6
FFN_RES = 0.5
A_RE_MAX = -1e-4
LOG2E = math.log2(math.e)
LANES = 128
HEAD_PAD = 128
V_EXTRA = 16
SSM_CHUNK = 16
SCAN_ROWS = 16
VMEM_LIMIT = 56 * 1024 * 1024
HIGHEST = lax.Precision.HIGHEST


def _dot(a, b, precision=None):
    return jnp.dot(a, b, preferred_element_type=F32, precision=precision)


def _rms(x, g):
    ms = jnp.mean(x * x, axis=-1, keepdims=True)
    return x * lax.rsqrt(ms + RMS_EPS) * g


def _silu(x):
    return x * jax.nn.sigmoid(x)


def _params(n_axes, semantics="arbitrary"):
    return pltpu.CompilerParams(dimension_semantics=(semantics,) * n_axes,
                                vmem_limit_bytes=VMEM_LIMIT)


def _const_spec(shape):
    zeros = (0,) * len(shape)
    return pl.BlockSpec(shape, lambda *_: zeros, pipeline_mode=pl.Buffered(1))


def _mod_kernel(c_ref, w_ref, b_ref, o_ref):
    sc = _silu(c_ref[...]).astype(BF16)
    o_ref[0] = _dot(sc, w_ref[0].astype(BF16)) + b_ref[0]


def _mod_call(c, mod_w, mod_b):
    depth, d, n = mod_w.shape
    bsz = c.shape[0]
    rows = 8
    c_pad = jnp.zeros((rows, d), F32).at[:bsz].set(c)
    tn = n // 4 if n % (4 * LANES) == 0 else n
    out = pl.pallas_call(
        _mod_kernel,
        out_shape=jax.ShapeDtypeStruct((depth, rows, n), F32),
        grid=(depth, n // tn),
        in_specs=[pl.BlockSpec((rows, d), lambda l, j: (0, 0)),
                  pl.BlockSpec((1, d, tn), lambda l, j: (l, 0, j)),
                  pl.BlockSpec((1, 1, tn), lambda l, j: (l, 0, j))],
        out_specs=pl.BlockSpec((1, rows, tn), lambda l, j: (l, 0, j)),
        compiler_params=_params(2),
    )(c_pad, mod_w, mod_b.reshape(depth, 1, n))
    return out[:, :bsz]


def _mod_rows(mod_ref, sub):
    shift = mod_ref[0, 3 * sub:3 * sub + 1, :]
    scale = mod_ref[0, 3 * sub + 1:3 * sub + 2, :]
    gate = mod_ref[0, 3 * sub + 2:3 * sub + 3, :]
    return shift, scale, gate


def _ffn_kernel(x_ref, mod_ref, gpre_ref, gpost_ref, wi_ref, wo_ref, o_ref, acc_ref, *, sub, tk):
    x = x_ref[...]
    shift, scale, gate = _mod_rows(mod_ref, sub)
    h = (_rms(x, gpre_ref[...]) * (1.0 + scale) + shift).astype(BF16)
    d_ff = wo_ref.shape[0]
    for k in range(d_ff // tk):
        g = _dot(h, wi_ref[:, k * tk:(k + 1) * tk].astype(BF16))
        u = _dot(h, wi_ref[:, d_ff + k * tk:d_ff + (k + 1) * tk].astype(BF16))
        a = (_silu(g) * u).astype(BF16)
        contrib = _dot(a, wo_ref[k * tk:(k + 1) * tk, :].astype(BF16))
        if k == 0:
            acc_ref[...] = contrib
        else:
            acc_ref[...] += contrib
    o_ref[...] = x + (FFN_RES * gate) * _rms(acc_ref[...], gpost_ref[...])


def _ffn_call(x, mod_l, g_pre, g_post, w_in_all, w_out_all, layer, which, *, sub, seq, tm):
    t, d = x.shape
    d_ff = w_out_all.shape[2]
    tk = 256 if d_ff % 256 == 0 else d_ff
    per_b = seq // tm
    row = lambda i: (i, 0)
    pick = lambda i: (layer, which, 0, 0)
    return pl.pallas_call(
        functools.partial(_ffn_kernel, sub=sub, tk=tk),
        out_shape=jax.ShapeDtypeStruct((t, d), F32),
        grid=(t // tm,),
        in_specs=[pl.BlockSpec((tm, d), row),
                  pl.BlockSpec((1, 9, d), lambda i: (i // per_b, 0, 0)),
                  _const_spec((1, d)), _const_spec((1, d)),
                  pl.BlockSpec((None, None, d, 2 * d_ff), pick, pipeline_mode=pl.Buffered(1)),
                  pl.BlockSpec((None, None, d_ff, d), pick, pipeline_mode=pl.Buffered(1))],
        out_specs=pl.BlockSpec((tm, d), row),
        scratch_shapes=[pltpu.VMEM((tm, d), F32)],
        compiler_params=_params(1),
    )(x, mod_l, g_pre.reshape(1, d), g_post.reshape(1, d), w_in_all, w_out_all)


def _split3(c):
    hi = c.astype(BF16).astype(F32)
    r = c - hi
    mid = r.astype(BF16).astype(F32)
    lo = (r - mid).astype(BF16).astype(F32)
    return hi, mid, lo


def _inproj_kernel(x_ref, mod_ref, gpre_ref, fb_ref, wu_ref, wq_ref, wk_ref, wv_ref, wf_ref,
                   wga_ref, wgb_ref, u_ref, q_ref, k_ref, vt_ref, ga_ref, gb_ref, carry_ref,
                   *, heads, dh, tk):
    tm = x_ref.shape[0]

    @pl.when(pl.program_id(1) == 0)
    def _():
        carry_ref[...] = jnp.zeros_like(carry_ref)

    shift, scale, _ = _mod_rows(mod_ref, 1)
    h = (_rms(x_ref[...], gpre_ref[...]) * (1.0 + scale) + shift).astype(BF16)
    u_ref[...] = _dot(h, wu_ref[...])
    ga_ref[...] = jax.nn.sigmoid(_dot(h, wga_ref[...])).astype(BF16)
    gb_ref[...] = jax.nn.sigmoid(_dot(h, wgb_ref[...])).astype(BF16)

    v_t = _dot(h, wv_ref[...]).T.astype(BF16)
    ones_rows = (lax.broadcasted_iota(jnp.int32, (V_EXTRA, tk), 0) == 0).astype(BF16)
    for hd in range(heads):
        r0 = (hd % 2) * (dh + V_EXTRA)
        for blk in range(tm // tk):
            vt_ref[hd // 2, blk, r0:r0 + dh, :] = v_t[hd * dh:(hd + 1) * dh, blk * tk:(blk + 1) * tk]
            vt_ref[hd // 2, blk, r0 + dh:r0 + dh + V_EXTRA, :] = ones_rows

    lane = lax.broadcasted_iota(jnp.int32, (tm, LANES), 1)
    log_f = jax.nn.log_sigmoid(_dot(h, wf_ref[...]) + fb_ref[...])
    hi, mid, lo = _split3(jnp.where(lane < heads, log_f, 0.0))
    packed = hi + pltpu.roll(mid, heads, axis=1) + pltpu.roll(lo, 2 * heads, axis=1)
    r = lax.broadcasted_iota(jnp.int32, (tm, tm), 0)
    c = lax.broadcasted_iota(jnp.int32, (tm, tm), 1)
    part = _dot((c <= r).astype(BF16), packed.astype(BF16))
    part = part + pltpu.roll(part, LANES - heads, axis=1) + pltpu.roll(part, LANES - 2 * heads, axis=1)
    cum = jnp.where(lane < heads, part, 0.0) + carry_ref[...]
    carry_ref[...] = cum[tm - 1:tm, :]

    q = _dot(h, wq_ref[...])
    k = _dot(h, wk_ref[...])
    ones_q = jnp.where((lane >= dh) & (lane < dh + 3), 1.0, 0.0)
    ones_k = jnp.where((lane >= dh + 3) & (lane < dh + 6), 1.0, 0.0)
    for hd in range(heads):
        hi, mid, lo = _split3(cum[:, hd:hd + 1] * LOG2E)
        q_aug = jnp.where(lane == dh + 3, hi, jnp.where(lane == dh + 4, mid,
                          jnp.where(lane == dh + 5, lo, ones_q)))
        k_aug = jnp.where(lane == dh, -hi, jnp.where(lane == dh + 1, -mid,
                          jnp.where(lane == dh + 2, -lo, ones_k)))
        src = slice((hd // 2) * LANES, (hd // 2 + 1) * LANES)
        move = (lambda a: a) if hd % 2 == 0 else (lambda a: pltpu.roll(a, dh, axis=1))
        dst = slice(hd * HEAD_PAD, (hd + 1) * HEAD_PAD)
        q_ref[:, dst] = jnp.where(lane < dh, move(q[:, src]), q_aug).astype(BF16)
        k_ref[:, dst] = jnp.where(lane < dh, move(k[:, src]), k_aug).astype(BF16)


def _inproj_call(x, mod_l, g_pre, w_in, forget_b, *, heads, dh, ssm_w, bsz, seq, tm, tk):
    t, d = x.shape
    aw = heads * dh
    w = w_in
    o = 0
    wu = w[:, o:o + ssm_w]; o += ssm_w
    wq = w[:, o:o + aw]; o += aw
    wk = w[:, o:o + aw]; o += aw
    wv = w[:, o:o + aw]; o += aw
    wf = w[:, o:o + heads]; o += heads
    wga = w[:, o:o + d]; o += d
    wgb = w[:, o:o + d]
    wq = wq * (dh ** -0.5 * LOG2E)
    wf = jnp.pad(wf, ((0, 0), (0, LANES - heads)))
    ws = [a.astype(BF16) for a in (wu, wq, wk, wv, wf, wga, wgb)]
    fb = jnp.pad(forget_b.reshape(1, heads), ((0, 0), (0, LANES - heads)))
    per_b = seq // tm
    row = lambda b, i: (b * per_b + i, 0)
    hp = heads * HEAD_PAD
    tok = lambda n, dt: (jax.ShapeDtypeStruct((t, n), dt), pl.BlockSpec((tm, n), row))
    vrows = 2 * (dh + V_EXTRA)
    vt = (jax.ShapeDtypeStruct((bsz, heads // 2, seq // tk, vrows, tk), BF16),
          pl.BlockSpec((None, heads // 2, tm // tk, vrows, tk), lambda b, i: (b, 0, i, 0, 0)))
    outs = [tok(ssm_w, F32), tok(hp, BF16), tok(hp, BF16), vt, tok(d, BF16), tok(d, BF16)]
    return pl.pallas_call(
        functools.partial(_inproj_kernel, heads=heads, dh=dh, tk=tk),
        out_shape=[o[0] for o in outs],
        grid=(bsz, per_b),
        in_specs=[pl.BlockSpec((tm, d), row),
                  pl.BlockSpec((1, 9, d), lambda b, i: (b, 0, 0)),
                  _const_spec((1, d)), _const_spec((1, LANES))] + [_const_spec(a.shape) for a in ws],
        out_specs=[o[1] for o in outs],
        scratch_shapes=[pltpu.VMEM((1, LANES), F32)],
        compiler_params=_params(2),
    )(x, mod_l, g_pre.reshape(1, d), fb, *ws)


def _attn_kernel(q_ref, k_ref, vt_ref, o_ref, s_ref, mx_ref, m_ref, acc_ref, *, tq, tk, dh):
    i = pl.program_id(2)
    assert tq == 2 * tk
    key =lax.broadcasted_iota(jnp.int32, (tk, tq), 0)
    qry = lax.broadcasted_iota(jnp.int32, (tk, tq), 1)
    head_cols = [slice(hh * HEAD_PAD, (hh + 1) * HEAD_PAD) for hh in range(2)]
    qs = [q_ref[:, hs] for hs in head_cols]

    def scores(j, slot, diag):
        rows = pl.ds(pl.multiple_of(j * tk, tk), tk)
        for hh in range(2):
            st = lax.dot_general(k_ref[rows, head_cols[hh]], qs[hh], (((1,), (1,)), ((), ())),
                                 preferred_element_type=F32)
            if diag is not None:
                st = jnp.where(key + diag * tk <= qry, st, -jnp.inf)
            s_ref[slot, hh] = st
            mx_ref[slot, hh] = jnp.max(st, axis=0, keepdims=True)

    vrows = dh + V_EXTRA

    def softmax_pv(j, slot):
        for hh in range(2):
            m = m_ref[hh]
            m_new = jnp.maximum(m, mx_ref[slot, hh])
            alpha = jnp.exp2(m - m_new)
            p = jnp.exp2(s_ref[slot, hh] - m_new)
            acc_ref[hh] = alpha * acc_ref[hh] + _dot(vt_ref[j, hh * vrows:(hh + 1) * vrows, :],
                                                     p.astype(BF16))
            m_ref[hh] = m_new

    d0, d1 = 2 * i, 2 * i + 1
    m_ref[...] = jnp.full(m_ref.shape, -jnp.inf, F32)
    acc_ref[...] = jnp.zeros(acc_ref.shape, F32)
    scores(d0, 0, 0)

    def pair(t):
        scores(2 * t, 1, None)
        softmax_pv(jnp.where(t == 0, d0, 2 * t - 1), 0)
        scores(2 * t + 1, 0, None)
        softmax_pv(2 * t, 1)

    def two_pairs(u, carry):
        pair(2 * u)
        pair(2 * u + 1)
        return carry

    lax.fori_loop(0, lax.shift_right_logical(i, 1), two_pairs, 0)

    @pl.when((i & 1) == 1)
    def _():
        pair(i - 1)

    scores(d1, 1, 1)
    softmax_pv(jnp.where(i == 0, d0, 2 * i - 1), 0)
    softmax_pv(d1, 1)
    o_t = jnp.concatenate([acc_ref[hh, :dh] / acc_ref[hh, dh:dh + 1] for hh in range(2)], axis=0)
    o_ref[...] = o_t.T.astype(BF16)


def _attn_call(q_aug, k_aug, vt, *, heads, dh, bsz, seq, tq, tk):
    t = q_aug.shape[0]
    nq = seq // tq
    nkv = seq // tk
    pair = 2 * HEAD_PAD
    return pl.pallas_call(
        functools.partial(_attn_kernel, tq=tq, tk=tk, dh=dh),
        out_shape=jax.ShapeDtypeStruct((t, heads * dh), BF16),
        grid=(bsz, heads // 2, nq),
        in_specs=[pl.BlockSpec((tq, pair), lambda b, h, i: (b * nq + i, h)),
                  pl.BlockSpec((seq, pair), lambda b, h, i: (b, h)),
                  pl.BlockSpec((None, None, nkv, vt.shape[3], tk), lambda b, h, i: (b, h, 0, 0, 0))],
        out_specs=pl.BlockSpec((tq, 2 * dh), lambda b, h, i: (b * nq + i, h)),
        scratch_shapes=[pltpu.VMEM((2, 2, tk, tq), F32),
                        pltpu.VMEM((2, 2, 1, tq), F32),
                        pltpu.VMEM((2, 1, tq), F32),
                        pltpu.VMEM((2, dh + V_EXTRA, tq), F32)],
        compiler_params=_params(3),
    )(q_aug, k_aug, vt)


def _s5_prep_kernel(re_c_ref, im_c_ref, re_r_ref, im_r_ref, ldt_ref, ct_re_ref, ct_im_ref,
                    bt_re_ref, bt_im_ref, d_ref,
                    mst_ref, ws_ref, wo_ref, a1_ref, a2_ref, b_scr, *, n_ch, p_st):
    cn = SSM_CHUNK * n_ch
    p2 = 2 * p_st
    gpb = LANES // n_ch
    assert cn == 2 * LANES and p2 == LANES
    shift_bits = n_ch.bit_length() - 1
    tau = lax.shift_right_logical(lax.broadcasted_iota(jnp.int32, (p2, cn), 1), shift_bits)
    top = lax.broadcasted_iota(jnp.int32, (p2, cn), 0) < p_st
    lane_b = lax.broadcasted_iota(jnp.int32, (n_ch, p2), 1) < p_st
    zr = lax.broadcasted_iota(jnp.int32, (n_ch, cn), 0)
    zl = lax.broadcasted_iota(jnp.int32, (n_ch, cn), 1)
    srow = lax.shift_right_logical(lax.broadcasted_iota(jnp.int32, (cn, p2), 0), shift_bits)
    lane_s = lax.broadcasted_iota(jnp.int32, (cn, p2), 1) < p_st
    lane_a = lax.broadcasted_iota(jnp.int32, (1, p2), 1) < p_st
    grp_z = lax.shift_right_logical(lax.broadcasted_iota(jnp.int32, (n_ch, LANES), 1), shift_bits)
    grp_r = lax.shift_right_logical(lax.broadcasted_iota(jnp.int32, (p2, LANES), 1), shift_bits)
    spread = (lax.broadcasted_iota(jnp.int32, (n_ch, cn), 0)
              == (lax.broadcasted_iota(jnp.int32, (n_ch, cn), 1) & (n_ch - 1))).astype(F32)

    ws_ref[0] = jnp.zeros(ws_ref.shape[1:], BF16)
    for g in range(gpb):
        dt = jnp.exp(ldt_ref[g])
        are_c = jnp.minimum(re_c_ref[g], A_RE_MAX) * dt
        aim_c = im_c_ref[g] * dt
        re_r = jnp.minimum(re_r_ref[g], A_RE_MAX)
        im_r = im_r_ref[g]
        are_r = re_r * dt
        aim_r = im_r * dt

        mag = jnp.exp(are_r)
        lb_re = mag * jnp.cos(aim_r)
        lb_im = mag * jnp.sin(aim_r)
        den = re_r * re_r + im_r * im_r
        xr = lb_re - 1.0
        coef_re = (xr * re_r + lb_im * im_r) / den
        coef_im = (lb_im * re_r - xr * im_r) / den

        ct_re = _dot(ct_re_ref[g], spread, HIGHEST)
        ct_im = _dot(ct_im_ref[g], spread, HIGHEST)

        def lam_pow_c(off):
            e = (tau + off).astype(F32)
            pm = jnp.exp(e * are_c)
            pr = pm * jnp.cos(e * aim_c)
            pi = pm * jnp.sin(e * aim_c)
            rr = pr * ct_re - pi * ct_im
            ri = pr * ct_im + pi * ct_re
            return jnp.where(top, rr, -ri)

        r1 = lam_pow_c(1)
        bb_re = coef_re * bt_re_ref[g] - coef_im * bt_im_ref[g]
        bb_im = coef_re * bt_im_ref[g] + coef_im * bt_re_ref[g]
        z = _dot(jnp.where(lane_b, bb_re, bb_im), lam_pow_c(0), HIGHEST)
        z = z + jnp.where(zr == zl, d_ref[g], 0.0)

        e = (SSM_CHUNK - 1 - srow).astype(F32)
        pm = jnp.exp(e * are_r)
        pr = pm * jnp.cos(e * aim_r)
        pi = pm * jnp.sin(e * aim_r)
        tb_re = jnp.concatenate([bb_re] * SSM_CHUNK, axis=0)
        tb_im = jnp.concatenate([bb_im] * SSM_CHUNK, axis=0)
        ws_g =jnp.where(lane_s, pr * tb_re - pi * tb_im, pr * tb_im + pi * tb_re).astype(BF16)

        for t in range(SSM_CHUNK):
            half = slice((t // gpb) * LANES, (t // gpb + 1) * LANES)
            shift = ((g - t % gpb) * n_ch) % LANES
            move = (lambda a: a) if shift == 0 else (lambda a: pltpu.roll(a, shift, axis=1))
            b_scr[t, g * n_ch:(g + 1) * n_ch, :] = jnp.where(grp_z == g, move(z[:, half]), 0.0)
            wo_ref[0, g * p2:(g + 1) * p2, t * LANES:(t + 1) * LANES] = (
                jnp.where(grp_r == g, move(r1[:, half]), 0.0).astype(BF16))
            ws_ref[0, t * LANES + g * n_ch:t * LANES + (g + 1) * n_ch, g * p2:(g + 1) * p2] = (
                ws_g[t * n_ch:(t + 1) * n_ch, :])

        ar, ai = lb_re, lb_im
        for _ in range(SSM_CHUNK.bit_length() - 1):
            ar, ai = ar * ar - ai * ai, 2.0 * ar * ai
        for k in range(SCAN_ROWS):
            a1_ref[0, k:k + 1, g * p2:(g + 1) * p2] = ar
            a2_ref[0, k:k + 1, g * p2:(g + 1) * p2] = jnp.where(lane_a, -ai, ai)
            ar, ai = ar * ar - ai * ai, 2.0 * ar * ai

    pairs = SSM_CHUNK // 2
    for d in range(pairs):
        r0 = (pairs - 1 - d) * 2 * LANES
        diag = b_scr[2 * d].astype(BF16)
        below = b_scr[2 * d - 1].astype(BF16) if d > 0 else jnp.zeros((LANES, LANES), BF16)
        mst_ref[0, r0:r0 + LANES, :LANES] = diag
        mst_ref[0, r0:r0 + LANES, LANES:] = b_scr[2 * d + 1].astype(BF16)
        mst_ref[0, r0 + LANES:r0 + 2 * LANES, :LANES] = below
        mst_ref[0, r0 + LANES:r0 + 2 * LANES, LANES:] = diag


def _s5_prep_call(a_re, a_im, log_dt, b_re, b_im, c_re, c_im, d_skip):
    p_st, n_ch = a_re.shape[-1], b_re.shape[-1]
    flat = lambda a, k: a.reshape((-1,) + a.shape[a.ndim - k:])
    a_re, a_im, log_dt = flat(a_re, 1), flat(a_im, 1), flat(log_dt, 0)
    b_re, b_im, c_re, c_im = flat(b_re, 2), flat(b_im, 2), flat(c_re, 2), flat(c_im, 2)
    g = a_re.shape[0]
    cn = SSM_CHUNK * n_ch
    p2 = 2 * p_st
    gpb = LANES // n_ch
    nblk = g // gpb
    assert g % gpb == 0
    col = lambda a: jnp.tile(a[:, :, None], (1, 2, 1))
    rowv = lambda a: jnp.tile(a[:, None, :], (1, 1, 2))
    ct = lambda a: jnp.tile(a.transpose(0, 2, 1), (1, 2, 1))
    bt = lambda a: jnp.tile(a.transpose(0, 2, 1), (1, 1, 2))
    d_pad = jnp.pad(d_skip.reshape(g, 1, n_ch), ((0, 0), (0, 0), (0, cn - n_ch)))
    args = (col(a_re), col(a_im), rowv(a_re), rowv(a_im), log_dt.reshape(g, 1, 1),
            ct(c_re), ct(c_im), bt(b_re), bt(b_im), d_pad)
    out_shapes = [jax.ShapeDtypeStruct((nblk, SSM_CHUNK * LANES, 2 * LANES), BF16),
                  jax.ShapeDtypeStruct((nblk, SSM_CHUNK * LANES, gpb * p2), BF16),
                  jax.ShapeDtypeStruct((nblk, gpb * p2, SSM_CHUNK * LANES), BF16),
                  jax.ShapeDtypeStruct((nblk, SCAN_ROWS, gpb * p2), F32),
                  jax.ShapeDtypeStruct((nblk, SCAN_ROWS, gpb * p2), F32)]
    return pl.pallas_call(
        functools.partial(_s5_prep_kernel, n_ch=n_ch, p_st=p_st),
        out_shape=out_shapes,
        grid=(nblk,),
        in_specs=[pl.BlockSpec((gpb,) + a.shape[1:], lambda i: (i, 0, 0)) for a in args],
        out_specs=[pl.BlockSpec((1,) + s.shape[1:], lambda i: (i, 0, 0)) for s in out_shapes],
        scratch_shapes=[pltpu.VMEM((SSM_CHUNK, LANES, LANES), F32)],
        compiler_params=_params(1),
    )(*args)


def _s5_kernel(u_ref, mst_ref, ws_ref, wo_ref, a1_ref, a2_ref, y_ref, xx_ref, sin_ref, *, p_st):
    ncb = xx_ref.shape[0]
    p2 = 2 * p_st
    pair = 2 * LANES
    pairs = SSM_CHUNK // 2
    for t in range(SSM_CHUNK):
        xx_ref[:, t * LANES:(t + 1) * LANES] = u_ref[pl.ds(t, ncb, stride=SSM_CHUNK), :].astype(BF16)
    s_all = _dot(xx_ref[...], ws_ref[0])
    row = lax.broadcasted_iota(jnp.int32, (ncb, p2), 0)
    for g in range(s_all.shape[1] // p2):
        cols = slice(g * p2, (g + 1) * p2)
        x = s_all[:, cols]
        d, k = 1, 0
        while d < ncb:
            xs = jnp.where(row >= d, pltpu.roll(x, d, axis=0), 0.0)
            x = x + a1_ref[0, k:k + 1, cols] * xs + a2_ref[0, k:k + 1, cols] * pltpu.roll(xs, p_st, axis=1)
            d, k = 2 * d, k + 1
        sin_ref[:, cols] = jnp.where(row >= 1, pltpu.roll(x, 1, axis=0), 0.0).astype(BF16)
    for tp in range(pairs):
        yy = (_dot(xx_ref[:, :pair * (tp + 1)], mst_ref[0, (pairs - 1 - tp) * pair:, :])
              + _dot(sin_ref[...], wo_ref[0, :, tp * pair:(tp + 1) * pair]))
        y_ref[pl.ds(2 * tp, ncb, stride=SSM_CHUNK), :] = yy[:, :LANES]
        y_ref[pl.ds(2 * tp + 1, ncb, stride=SSM_CHUNK), :] = yy[:, LANES:]


def _s5_call(u, mats, layer, *, bsz, seq):
    mst, ws, wo, a1, a2 = mats
    t, ssm_w = u.shape
    nblk = ssm_w // LANES
    ncb = seq // SSM_CHUNK
    p_st = LANES // 2
    assert ncb < 2 ** SCAN_ROWS
    wspec = lambda a: pl.BlockSpec((1,) + a.shape[1:], lambda o, b: (layer * nblk + o, 0, 0))
    return pl.pallas_call(
        functools.partial(_s5_kernel, p_st=p_st),
        out_shape=jax.ShapeDtypeStruct((t, ssm_w), F32),
        grid=(nblk, bsz),
        in_specs=[pl.BlockSpec((seq, LANES), lambda o, b: (b, o))] + [wspec(a) for a in mats],
        out_specs=pl.BlockSpec((seq, LANES), lambda o, b: (b, o)),
        scratch_shapes=[pltpu.VMEM((ncb, SSM_CHUNK * LANES), BF16),
                        pltpu.VMEM((ncb, ws.shape[2]), BF16)],
        compiler_params=_params(2),
    )(u, *mats)


def _merge_kernel(x_ref, mod_ref, gpost_ref, ys_ref, at_ref, ga_ref, gb_ref,
                  wglu_ref, wat_ref, wout_ref, o_ref):
    d = x_ref.shape[1]
    _, _, gate = _mod_rows(mod_ref, 1)
    z = _dot(jax.nn.gelu(ys_ref[...]).astype(BF16), wglu_ref[...].astype(BF16))
    y_a = z[:, :d] * jax.nn.sigmoid(z[:, d:])
    y_b = _dot(at_ref[...], wat_ref[...].astype(BF16))
    merged = ga_ref[...].astype(F32) * y_a + gb_ref[...].astype(F32) * y_b
    y = _dot(merged.astype(BF16), wout_ref[...].astype(BF16))
    o_ref[...] = x_ref[...] + gate * _rms(y, gpost_ref[...])


def _merge_call(x, mod_l, g_post, y_ssm, attn, ga, gb, glu_w, attn_w_out, w_out, layer, *, seq, tm):
    t, d = x.shape
    per_b = seq // tm
    row = lambda i: (i, 0)
    ws = [glu_w, attn_w_out, w_out]
    acts = (y_ssm, attn, ga, gb)
    return pl.pallas_call(
        _merge_kernel,
        out_shape=jax.ShapeDtypeStruct((t, d), F32),
        grid=(t // tm,),
        in_specs=[pl.BlockSpec((tm, d), row),
                  pl.BlockSpec((1, 9, d), lambda i: (i // per_b, 0, 0)),
                  _const_spec((1, d))]
                 + [pl.BlockSpec((tm, a.shape[1]), row) for a in acts]
                 + [pl.BlockSpec((None,) + a.shape[1:], lambda i: (layer, 0, 0),
                                 pipeline_mode=pl.Buffered(1)) for a in ws],
        out_specs=pl.BlockSpec((tm, d), row),
        compiler_params=_params(1),
    )(x, mod_l, g_post.reshape(1, d), *acts, *ws)


def kernel(x, c, mod_w, mod_b, norm_pre, norm_post, ffn_w_in, ffn_w_out, mix_w_in, forget_b,
           ssm_a_re, ssm_a_im, ssm_log_dt, ssm_b_re, ssm_b_im, ssm_c_re, ssm_c_im, ssm_d,
           glu_w, attn_w_out, mix_w_out):
    bsz, seq, d = x.shape
    depth = mod_w.shape[0]
    heads = forget_b.shape[1]
    dh = attn_w_out.shape[1] // heads
    groups = ssm_a_re.shape[1]
    ssm_w = ssm_d.shape[1]
    assert heads % 2 == 0 and 2 * dh == LANES and seq % SSM_CHUNK == 0
    tm = min(512, seq)
    tq = min(512, seq)
    tk = tq // 2
    assert seq % tm == 0 and seq % tq == 0

    mod = _mod_call(c, mod_w, mod_b).reshape(depth, bsz, 9, d)
    mats = _s5_prep_call(ssm_a_re, ssm_a_im, ssm_log_dt, ssm_b_re, ssm_b_im, ssm_c_re, ssm_c_im, ssm_d)
    xt = x.reshape(bsz * seq, d)
    for l in range(depth):
        mod_l = mod[l]
        xt = _ffn_call(xt, mod_l, norm_pre[l, 0], norm_post[l, 0], ffn_w_in, ffn_w_out, l, 0,
                       sub=0, seq=seq, tm=tm)
        u, q_aug, k_aug, vt, ga, gb = _inproj_call(xt, mod_l, norm_pre[l, 1], mix_w_in[l], forget_b[l],
                                                   heads=heads, dh=dh, ssm_w=ssm_w, bsz=bsz, seq=seq,
                                                   tm=tm, tk=tk)
        attn = _attn_call(q_aug, k_aug, vt, heads=heads, dh=dh, bsz=bsz, seq=seq, tq=tq, tk=tk)
        y_ssm = _s5_call(u, mats, l, bsz=bsz, seq=seq)
        xt = _merge_call(xt, mod_l, norm_post[l, 1], y_ssm, attn, ga, gb, glu_w, attn_w_out,
                         mix_w_out, l, seq=seq, tm=tm)
        xt = _ffn_call(xt, mod_l, norm_pre[l, 2], norm_post[l, 2], ffn_w_in, ffn_w_out, l, 1,
                       sub=2, seq=seq, tm=tm)
    return xt.reshape(bsz, seq, d)
```

```python
import functools
import math

import jax
import jax.numpy as jnp
from jax import lax
from jax.experimental import pallas as pl
from jax.experimental.pallas import tpu as pltpu

F32 = jnp.float32
BF16 = jnp.bfloat16

RMS_EPS = 1e-6
FFN_RES = 0.5
A_RE_MAX = -1e-4
LOG2E = math.log2(math.e)
LANES = 128
HEAD_PAD = 128
V_EXTRA = 16
SSM_CHUNK = 16
SCAN_ROWS = 16
VMEM_LIMIT = 56 * 1024 * 1024
HIGHEST = lax.Precision.HIGHEST


def _dot(a, b, precision=None):
    return jnp.dot(a, b, preferred_element_type=F32, precision=precision)


def _rms(x, g):
    ms = jnp.mean(x * x, axis=-1, keepdims=True)
    return x * lax.rsqrt(ms + RMS_EPS) * g


def _silu(x):
    return x * jax.nn.sigmoid(x)


def _params(n_axes, semantics="arbitrary"):
    return pltpu.CompilerParams(dimension_semantics=(semantics,) * n_axes,
                                vmem_limit_bytes=VMEM_LIMIT)


def _const_spec(shape):
    zeros = (0,) * len(shape)
    return pl.BlockSpec(shape, lambda *_: zeros, pipeline_mode=pl.Buffered(1))


def _mod_kernel(c_ref, w_ref, b_ref, o_ref):
    sc = _silu(c_ref[...]).astype(BF16)
    o_ref[0] = _dot(sc, w_ref[0].astype(BF16)) + b_ref[0]


def _mod_call(c, mod_w, mod_b):
    depth, d, n = mod_w.shape
    bsz = c.shape[0]
    rows = 8
    c_pad = jnp.zeros((rows, d), F32).at[:bsz].set(c)
    tn = n // 8 if n % (8 * LANES) == 0 else n
    out = pl.pallas_call(
        _mod_kernel,
        out_shape=jax.ShapeDtypeStruct((depth, rows, n), F32),
        grid=(depth, n // tn),
        in_specs=[pl.BlockSpec((rows, d), lambda l, j: (0, 0)),
                  pl.BlockSpec((1, d, tn), lambda l, j: (l, 0, j)),
                  pl.BlockSpec((1, 1, tn), lambda l, j: (l, 0, j))],
        out_specs=pl.BlockSpec((1, rows, tn), lambda l, j: (l, 0, j)),
        compiler_params=_params(2),
    )(c_pad, mod_w, mod_b.reshape(depth, 1, n))
    return out[:, :bsz]


def _mod_rows(mod_ref, sub):
    shift = mod_ref[0, 3 * sub:3 * sub + 1, :]
    scale = mod_ref[0, 3 * sub + 1:3 * sub + 2, :]
    gate = mod_ref[0, 3 * sub + 2:3 * sub + 3, :]
    return shift, scale, gate


def _ffn_kernel(x_ref, mod_ref, gpre_ref, gpost_ref, wi_ref, wo_ref, o_ref, acc_ref, *, sub, tk):
    x = x_ref[...]
    shift, scale, gate = _mod_rows(mod_ref, sub)
    h = (_rms(x, gpre_ref[...]) * (1.0 + scale) + shift).astype(BF16)
    d_ff = wo_ref.shape[0]
    for k in range(d_ff // tk):
        g = _dot(h, wi_ref[:, k * tk:(k + 1) * tk].astype(BF16))
        u = _dot(h, wi_ref[:, d_ff + k * tk:d_ff + (k + 1) * tk].astype(BF16))
        a = (_silu(g) * u).astype(BF16)
        contrib = _dot(a, wo_ref[k * tk:(k + 1) * tk, :].astype(BF16))
        if k == 0:
            acc_ref[...] = contrib
        else:
            acc_ref[...] += contrib
    o_ref[...] = x + (FFN_RES * gate) * _rms(acc_ref[...], gpost_ref[...])


def _ffn_call(x, mod_l, g_pre, g_post, w_in_all, w_out_all, layer, which, *, sub, seq, tm):
    t, d = x.shape
    d_ff = w_out_all.shape[2]
    tk = 256 if d_ff % 256 == 0 else d_ff
    per_b = seq // tm
    row = lambda i: (i, 0)
    pick = lambda i: (layer, which, 0, 0)
    return pl.pallas_call(
        functools.partial(_ffn_kernel, sub=sub, tk=tk),
        out_shape=jax.ShapeDtypeStruct((t, d), F32),
        grid=(t // tm,),
        in_specs=[pl.BlockSpec((tm, d), row),
                  pl.BlockSpec((1, 9, d), lambda i: (i // per_b, 0, 0)),
                  _const_spec((1, d)), _const_spec((1, d)),
                  pl.BlockSpec((None, None, d, 2 * d_ff), pick, pipeline_mode=pl.Buffered(1)),
                  pl.BlockSpec((None, None, d_ff, d), pick, pipeline_mode=pl.Buffered(1))],
        out_specs=pl.BlockSpec((tm, d), row),
        scratch_shapes=[pltpu.VMEM((tm, d), F32)],
        compiler_params=_params(1),
    )(x, mod_l, g_pre.reshape(1, d), g_post.reshape(1, d), w_in_all, w_out_all)


def _split3(c):
    hi = c.astype(BF16).astype(F32)
    r = c - hi
    mid = r.astype(BF16).astype(F32)
    lo = (r - mid).astype(BF16).astype(F32)
    return hi, mid, lo


def _inproj_kernel(x_ref, mod_ref, gpre_ref, fb_ref, wu_ref, wq_ref, wk_ref, wv_ref, wf_ref,
                   wga_ref, wgb_ref, u_ref, q_ref, k_ref, vt_ref, ga_ref, gb_ref, carry_ref,
                   *, heads, dh, tk):
    tm = x_ref.shape[0]

    @pl.when(pl.program_id(1) == 0)
    def _():
        carry_ref[...] = jnp.zeros_like(carry_ref)

    shift, scale, _ = _mod_rows(mod_ref, 1)
    h = (_rms(x_ref[...], gpre_ref[...]) * (1.0 + scale) + shift).astype(BF16)

    lane = lax.broadcasted_iota(jnp.int32, (tm, LANES), 1)
    log_f = jax.nn.log_sigmoid(_dot(h, wf_ref[...]) + fb_ref[...])
    hi, mid, lo = _split3(jnp.where(lane < heads, log_f, 0.0))
    packed = hi + pltpu.roll(mid, heads, axis=1) + pltpu.roll(lo, 2 * heads, axis=1)
    r = lax.broadcasted_iota(jnp.int32, (tm, tm), 0)
    c = lax.broadcasted_iota(jnp.int32, (tm, tm), 1)
    part = _dot((c <= r).astype(BF16), packed.astype(BF16))
    part = part + pltpu.roll(part, LANES - heads, axis=1) + pltpu.roll(part, LANES - 2 * heads, axis=1)
    cum = jnp.where(lane < heads, part, 0.0) + carry_ref[...]
    carry_ref[...] = cum[tm - 1:tm, :]

    q = _dot(h, wq_ref[...])
    k = _dot(h, wk_ref[...])
    ones_q = jnp.where((lane >= dh) & (lane < dh + 3), 1.0, 0.0)
    ones_k = jnp.where((lane >= dh + 3) & (lane < dh + 6), 1.0, 0.0)
    for hd in range(heads):
        hi, mid, lo = _split3(cum[:, hd:hd + 1] * LOG2E)
        q_aug = jnp.where(lane == dh + 3, hi, jnp.where(lane == dh + 4, mid,
                          jnp.where(lane == dh + 5, lo, ones_q)))
        k_aug = jnp.where(lane == dh, -hi, jnp.where(lane == dh + 1, -mid,
                          jnp.where(lane == dh + 2, -lo, ones_k)))
        src = slice((hd // 2) * LANES, (hd // 2 + 1) * LANES)
        move = (lambda a: a) if hd % 2 == 0 else (lambda a: pltpu.roll(a, dh, axis=1))
        dst = slice(hd * HEAD_PAD, (hd + 1) * HEAD_PAD)
        q_ref[:, dst] = jnp.where(lane < dh, move(q[:, src]), q_aug).astype(BF16)
        k_ref[:, dst] = jnp.where(lane < dh, move(k[:, src]), k_aug).astype(BF16)

    v_t = _dot(h, wv_ref[...]).T.astype(BF16)
    ones_rows = (lax.broadcasted_iota(jnp.int32, (V_EXTRA, tk), 0) == 0).astype(BF16)
    for hd in range(heads):
        r0 = (hd % 2) * (dh + V_EXTRA)
        for blk in range(tm // tk):
            vt_ref[hd // 2, blk, r0:r0 + dh, :] = v_t[hd * dh:(hd + 1) * dh, blk * tk:(blk + 1) * tk]
            vt_ref[hd // 2, blk, r0 + dh:r0 + dh + V_EXTRA, :] = ones_rows

    u_ref[...] = _dot(h, wu_ref[...])
    ga_ref[...] = jax.nn.sigmoid(_dot(h, wga_ref[...])).astype(BF16)
    gb_ref[...] = jax.nn.sigmoid(_dot(h, wgb_ref[...])).astype(BF16)


def _inproj_call(x, mod_l, g_pre, w_in, forget_b, *, heads, dh, ssm_w, bsz, seq, tm, tk):
    t, d = x.shape
    aw = heads * dh
    w = w_in
    o = 0
    wu = w[:, o:o + ssm_w]; o += ssm_w
    wq = w[:, o:o + aw]; o += aw
    wk = w[:, o:o + aw]; o += aw
    wv = w[:, o:o + aw]; o += aw
    wf = w[:, o:o + heads]; o += heads
    wga = w[:, o:o + d]; o += d
    wgb = w[:, o:o + d]
    wq = wq * (dh ** -0.5 * LOG2E)
    wf = jnp.pad(wf, ((0, 0), (0, LANES - heads)))
    ws = [a.astype(BF16) for a in (wu, wq, wk, wv, wf, wga, wgb)]
    fb = jnp.pad(forget_b.reshape(1, heads), ((0, 0), (0, LANES - heads)))
    per_b = seq // tm
    row = lambda b, i: (b * per_b + i, 0)
    hp = heads * HEAD_PAD
    tok = lambda n, dt: (jax.ShapeDtypeStruct((t, n), dt), pl.BlockSpec((tm, n), row))
    vrows = 2 * (dh + V_EXTRA)
    vt = (jax.ShapeDtypeStruct((bsz, heads // 2, seq // tk, vrows, tk), BF16),
          pl.BlockSpec((None, heads // 2, tm // tk, vrows, tk), lambda b, i: (b, 0, i, 0, 0)))
    outs = [tok(ssm_w, F32), tok(hp, BF16), tok(hp, BF16), vt, tok(d, BF16), tok(d, BF16)]
    return pl.pallas_call(
        functools.partial(_inproj_kernel, heads=heads, dh=dh, tk=tk),
        out_shape=[o[0] for o in outs],
        grid=(bsz, per_b),
        in_specs=[pl.BlockSpec((tm, d), row),
                  pl.BlockSpec((1, 9, d), lambda b, i: (b, 0, 0)),
                  _const_spec((1, d)), _const_spec((1, LANES))] + [_const_spec(a.shape) for a in ws],
        out_specs=[o[1] for o in outs],
        scratch_shapes=[pltpu.VMEM((1, LANES), F32)],
        compiler_params=_params(2),
    )(x, mod_l, g_pre.reshape(1, d), fb, *ws)


def _attn_kernel(q_ref, k_ref, vt_ref, o_ref, s_ref, mx_ref, m_ref, acc_ref, *, tq, tk, dh):
    i = pl.program_id(2)
    assert tq == 2 * tk
    key =lax.broadcasted_iota(jnp.int32, (tk, tq), 0)
    qry = lax.broadcasted_iota(jnp.int32, (tk, tq), 1)
    head_cols = [slice(hh * HEAD_PAD, (hh + 1) * HEAD_PAD) for hh in range(2)]
    qs = [q_ref[:, hs] for hs in head_cols]

    def scores(j, slot, diag):
        rows = pl.ds(pl.multiple_of(j * tk, tk), tk)
        for hh in range(2):
            st = lax.dot_general(k_ref[rows, head_cols[hh]], qs[hh], (((1,), (1,)), ((), ())),
                                 preferred_element_type=F32)
            if diag is not None:
                st = jnp.where(key + diag * tk <= qry, st, -jnp.inf)
            s_ref[slot, hh] = st
            mx_ref[slot, hh] = jnp.max(st, axis=0, keepdims=True)

    vrows = dh + V_EXTRA

    def softmax_pv(j, slot):
        for hh in range(2):
            m = m_ref[hh]
            m_new = jnp.maximum(m, mx_ref[slot, hh])
            alpha = jnp.exp2(m - m_new)
            p = jnp.exp2(s_ref[slot, hh] - m_new)
            acc_ref[hh] = alpha * acc_ref[hh] + _dot(vt_ref[j, hh * vrows:(hh + 1) * vrows, :],
                                                     p.astype(BF16))
            m_ref[hh] = m_new

    d0, d1 = 2 * i, 2 * i + 1
    m_ref[...] = jnp.full(m_ref.shape, -jnp.inf, F32)
    acc_ref[...] = jnp.zeros(acc_ref.shape, F32)
    scores(d0, 0, 0)

    def pair(t):
        scores(2 * t, 1, None)
        softmax_pv(jnp.where(t == 0, d0, 2 * t - 1), 0)
        scores(2 * t + 1, 0, None)
        softmax_pv(2 * t, 1)

    def two_pairs(u, carry):
        pair(2 * u)
        pair(2 * u + 1)
        return carry

    lax.fori_loop(0, lax.shift_right_logical(i, 1), two_pairs, 0)

    @pl.when((i & 1) == 1)
    def _():
        pair(i - 1)

    scores(d1, 1, 1)
    softmax_pv(jnp.where(i == 0, d0, 2 * i - 1), 0)
    softmax_pv(d1, 1)
    o_t = jnp.concatenate([acc_ref[hh, :dh] / acc_ref[hh, dh:dh + 1] for hh in range(2)], axis=0)
    o_ref[...] = o_t.T.astype(BF16)


def _attn_call(q_aug, k_aug, vt, *, heads, dh, bsz, seq, tq, tk):
    t = q_aug.shape[0]
    nq = seq // tq
    nkv = seq // tk
    pair = 2 * HEAD_PAD
    return pl.pallas_call(
        functools.partial(_attn_kernel, tq=tq, tk=tk, dh=dh),
        out_shape=jax.ShapeDtypeStruct((t, heads * dh), BF16),
        grid=(bsz, heads // 2, nq),
        in_specs=[pl.BlockSpec((tq, pair), lambda b, h, i: (b * nq + i, h)),
                  pl.BlockSpec((seq, pair), lambda b, h, i: (b, h)),
                  pl.BlockSpec((None, None, nkv, vt.shape[3], tk), lambda b, h, i: (b, h, 0, 0, 0))],
        out_specs=pl.BlockSpec((tq, 2 * dh), lambda b, h, i: (b * nq + i, h)),
        scratch_shapes=[pltpu.VMEM((2, 2, tk, tq), F32),
                        pltpu.VMEM((2, 2, 1, tq), F32),
                        pltpu.VMEM((2, 1, tq), F32),
                        pltpu.VMEM((2, dh + V_EXTRA, tq), F32)],
        compiler_params=_params(3),
    )(q_aug, k_aug, vt)


def _int_power(re, im, e, e_max):
    pr, pi = jnp.ones(e.shape, F32), jnp.zeros(e.shape, F32)
    bit = 0
    while (1 << bit) <= e_max:
        on = (lax.shift_right_logical(e, bit) & 1) == 1
        pr, pi = jnp.where(on, pr * re - pi * im, pr), jnp.where(on, pr * im + pi * re, pi)
        re, im = re * re - im * im, 2.0 * re * im
        bit += 1
    return pr, pi


def _s5_prep_kernel(re_c_ref, im_c_ref, re_r_ref, im_r_ref, ldt_ref, ct_re_ref, ct_im_ref,
                    bt_re_ref, bt_im_ref, d_ref,
                    mst_ref, ws_ref, wo_ref, a1_ref, a2_ref, b_scr, *, n_ch, p_st):
    cn = SSM_CHUNK * n_ch
    p2 = 2 * p_st
    gpb = LANES // n_ch
    assert cn == 2 * LANES and p2 == LANES
    shift_bits = n_ch.bit_length() - 1
    tau = lax.shift_right_logical(lax.broadcasted_iota(jnp.int32, (p2, cn), 1), shift_bits)
    top = lax.broadcasted_iota(jnp.int32, (p2, cn), 0) < p_st
    lane_b = lax.broadcasted_iota(jnp.int32, (n_ch, p2), 1) < p_st
    zr = lax.broadcasted_iota(jnp.int32, (n_ch, cn), 0)
    zl = lax.broadcasted_iota(jnp.int32, (n_ch, cn), 1)
    srow = lax.shift_right_logical(lax.broadcasted_iota(jnp.int32, (cn, p2), 0), shift_bits)
    lane_s = lax.broadcasted_iota(jnp.int32, (cn, p2), 1) < p_st
    lane_a = lax.broadcasted_iota(jnp.int32, (1, p2), 1) < p_st
    grp_z = lax.shift_right_logical(lax.broadcasted_iota(jnp.int32, (n_ch, LANES), 1), shift_bits)
    grp_r = lax.shift_right_logical(lax.broadcasted_iota(jnp.int32, (p2, LANES), 1), shift_bits)
    spread = (lax.broadcasted_iota(jnp.int32, (n_ch, cn), 0)
              == (lax.broadcasted_iota(jnp.int32, (n_ch, cn), 1) & (n_ch - 1))).astype(F32)

    ws_ref[0] = jnp.zeros(ws_ref.shape[1:], BF16)
    for g in range(gpb):
        dt = jnp.exp(ldt_ref[g])
        are_c = jnp.minimum(re_c_ref[g], A_RE_MAX) * dt
        aim_c = im_c_ref[g] * dt
        re_r = jnp.minimum(re_r_ref[g], A_RE_MAX)
        im_r = im_r_ref[g]
        are_r = re_r * dt
        aim_r = im_r * dt

        mag = jnp.exp(are_r)
        lb_re = mag * jnp.cos(aim_r)
        lb_im = mag * jnp.sin(aim_r)
        den = re_r * re_r + im_r * im_r
        xr = lb_re - 1.0
        coef_re = (xr * re_r + lb_im * im_r) / den
        coef_im = (lb_im * re_r - xr * im_r) / den

        ct_re = _dot(ct_re_ref[g], spread, HIGHEST)
        ct_im = _dot(ct_im_ref[g], spread, HIGHEST)

        mag_c = jnp.exp(are_c)
        lam_c = (mag_c * jnp.cos(aim_c), mag_c * jnp.sin(aim_c))
        pw_re, pw_im = _int_power(lam_c[0], lam_c[1], tau, SSM_CHUNK - 1)

        def times_c(pr, pi):
            rr = pr * ct_re - pi * ct_im
            ri = pr * ct_im + pi * ct_re
            return jnp.where(top, rr, -ri)

        r1 = times_c(pw_re * lam_c[0] - pw_im * lam_c[1], pw_re * lam_c[1] + pw_im * lam_c[0])
        bb_re = coef_re * bt_re_ref[g] - coef_im * bt_im_ref[g]
        bb_im = coef_re * bt_im_ref[g] + coef_im * bt_re_ref[g]
        z = _dot(jnp.where(lane_b, bb_re, bb_im), times_c(pw_re, pw_im), HIGHEST)
        z = z + jnp.where(zr == zl, d_ref[g], 0.0)

        pr, pi = _int_power(lb_re, lb_im, SSM_CHUNK - 1 - srow, SSM_CHUNK - 1)
        tb_re = jnp.concatenate([bb_re] * SSM_CHUNK, axis=0)
        tb_im = jnp.concatenate([bb_im] * SSM_CHUNK, axis=0)
        ws_g =jnp.where(lane_s, pr * tb_re - pi * tb_im, pr * tb_im + pi * tb_re).astype(BF16)

        for t in range(SSM_CHUNK):
            half = slice((t // gpb) * LANES, (t // gpb + 1) * LANES)
            shift = ((g - t % gpb) * n_ch) % LANES
            move = (lambda a: a) if shift == 0 else (lambda a: pltpu.roll(a, shift, axis=1))
            b_scr[t, g * n_ch:(g + 1) * n_ch, :] = jnp.where(grp_z == g, move(z[:, half]), 0.0)
            wo_ref[0, g * p2:(g + 1) * p2, t * LANES:(t + 1) * LANES] = (
                jnp.where(grp_r == g, move(r1[:, half]), 0.0).astype(BF16))
            ws_ref[0, t * LANES + g * n_ch:t * LANES + (g + 1) * n_ch, g * p2:(g + 1) * p2] = (
                ws_g[t * n_ch:(t + 1) * n_ch, :])

        ar, ai = lb_re, lb_im
        for _ in range(SSM_CHUNK.bit_length() - 1):
            ar, ai = ar * ar - ai * ai, 2.0 * ar * ai
        for k in range(SCAN_ROWS):
            a1_ref[0, k:k + 1, g * p2:(g + 1) * p2] = ar
            a2_ref[0, k:k + 1, g * p2:(g + 1) * p2] = jnp.where(lane_a, -ai, ai)
            ar, ai = ar * ar - ai * ai, 2.0 * ar * ai

    pairs = SSM_CHUNK // 2
    for d in range(pairs):
        r0 = (pairs - 1 - d) * 2 * LANES
        diag = b_scr[2 * d].astype(BF16)
        below = b_scr[2 * d - 1].astype(BF16) if d > 0 else jnp.zeros((LANES, LANES), BF16)
        mst_ref[0, r0:r0 + LANES, :LANES] = diag
        mst_ref[0, r0:r0 + LANES, LANES:] = b_scr[2 * d + 1].astype(BF16)
        mst_ref[0, r0 + LANES:r0 + 2 * LANES, :LANES] = below
        mst_ref[0, r0 + LANES:r0 + 2 * LANES, LANES:] = diag


def _s5_prep_call(a_re, a_im, log_dt, b_re, b_im, c_re, c_im, d_skip):
    p_st, n_ch = a_re.shape[-1], b_re.shape[-1]
    flat = lambda a, k: a.reshape((-1,) + a.shape[a.ndim - k:])
    a_re, a_im, log_dt = flat(a_re, 1), flat(a_im, 1), flat(log_dt, 0)
    b_re, b_im, c_re, c_im = flat(b_re, 2), flat(b_im, 2), flat(c_re, 2), flat(c_im, 2)
    g = a_re.shape[0]
    cn = SSM_CHUNK * n_ch
    p2 = 2 * p_st
    gpb = LANES // n_ch
    nblk = g // gpb
    assert g % gpb == 0
    col = lambda a: jnp.tile(a[:, :, None], (1, 2, 1))
    rowv = lambda a: jnp.tile(a[:, None, :], (1, 1, 2))
    ct = lambda a: jnp.tile(a.transpose(0, 2, 1), (1, 2, 1))
    bt = lambda a: jnp.tile(a.transpose(0, 2, 1), (1, 1, 2))
    d_pad = jnp.pad(d_skip.reshape(g, 1, n_ch), ((0, 0), (0, 0), (0, cn - n_ch)))
    args = (col(a_re), col(a_im), rowv(a_re), rowv(a_im), log_dt.reshape(g, 1, 1),
            ct(c_re), ct(c_im), bt(b_re), bt(b_im), d_pad)
    out_shapes = [jax.ShapeDtypeStruct((nblk, SSM_CHUNK * LANES, 2 * LANES), BF16),
                  jax.ShapeDtypeStruct((nblk, SSM_CHUNK * LANES, gpb * p2), BF16),
                  jax.ShapeDtypeStruct((nblk, gpb * p2, SSM_CHUNK * LANES), BF16),
                  jax.ShapeDtypeStruct((nblk, SCAN_ROWS, gpb * p2), F32),
                  jax.ShapeDtypeStruct((nblk, SCAN_ROWS, gpb * p2), F32)]
    return pl.pallas_call(
        functools.partial(_s5_prep_kernel, n_ch=n_ch, p_st=p_st),
        out_shape=out_shapes,
        grid=(nblk,),
        in_specs=[pl.BlockSpec((gpb,) + a.shape[1:], lambda i: (i, 0, 0)) for a in args],
        out_specs=[pl.BlockSpec((1,) + s.shape[1:], lambda i: (i, 0, 0)) for s in out_shapes],
        scratch_shapes=[pltpu.VMEM((SSM_CHUNK, LANES, LANES), F32)],
        compiler_params=_params(1),
    )(*args)


def _s5_kernel(u_ref, mst_ref, ws_ref, wo_ref, a1_ref, a2_ref, y_ref, xx_ref, sin_ref, *, p_st):
    ncb = xx_ref.shape[0]
    p2 = 2 * p_st
    pair = 2 * LANES
    pairs = SSM_CHUNK // 2
    for t in range(SSM_CHUNK):
        xx_ref[:, t * LANES:(t + 1) * LANES] = u_ref[pl.ds(t, ncb, stride=SSM_CHUNK), :].astype(BF16)
    s_all = _dot(xx_ref[...], ws_ref[0])
    row = lax.broadcasted_iota(jnp.int32, (ncb, p2), 0)
    for g in range(s_all.shape[1] // p2):
        cols = slice(g * p2, (g + 1) * p2)
        x = s_all[:, cols]
        d, k = 1, 0
        while d < ncb:
            xs = jnp.where(row >= d, pltpu.roll(x, d, axis=0), 0.0)
            x = x + a1_ref[0, k:k + 1, cols] * xs + a2_ref[0, k:k + 1, cols] * pltpu.roll(xs, p_st, axis=1)
            d, k = 2 * d, k + 1
        sin_ref[:, cols] = jnp.where(row >= 1, pltpu.roll(x, 1, axis=0), 0.0).astype(BF16)
    for tp in range(pairs):
        yy = (_dot(xx_ref[:, :pair * (tp + 1)], mst_ref[0, (pairs - 1 - tp) * pair:, :])
              + _dot(sin_ref[...], wo_ref[0, :, tp * pair:(tp + 1) * pair]))
        y_ref[pl.ds(2 * tp, ncb, stride=SSM_CHUNK), :] = yy[:, :LANES]
        y_ref[pl.ds(2 * tp + 1, ncb, stride=SSM_CHUNK), :] = yy[:, LANES:]


def _s5_call(u, mats, layer, *, bsz, seq):
    mst, ws, wo, a1, a2 = mats
    t, ssm_w = u.shape
    nblk = ssm_w // LANES
    ncb = seq // SSM_CHUNK
    p_st = LANES // 2
    assert ncb < 2 ** SCAN_ROWS
    wspec = lambda a: pl.BlockSpec((1,) + a.shape[1:], lambda o, b: (layer * nblk + o, 0, 0))
    return pl.pallas_call(
        functools.partial(_s5_kernel, p_st=p_st),
        out_shape=jax.ShapeDtypeStruct((t, ssm_w), F32),
        grid=(nblk, bsz),
        in_specs=[pl.BlockSpec((seq, LANES), lambda o, b: (b, o))] + [wspec(a) for a in mats],
        out_specs=pl.BlockSpec((seq, LANES), lambda o, b: (b, o)),
        scratch_shapes=[pltpu.VMEM((ncb, SSM_CHUNK * LANES), BF16),
                        pltpu.VMEM((ncb, ws.shape[2]), BF16)],
        compiler_params=_params(2),
    )(u, *mats)


def _merge_kernel(x_ref, mod_ref, gpost_ref, ys_ref, at_ref, ga_ref, gb_ref,
                  wglu_ref, wat_ref, wout_ref, o_ref):
    d = x_ref.shape[1]
    _, _, gate = _mod_rows(mod_ref, 1)
    z = _dot(jax.nn.gelu(ys_ref[...]).astype(BF16), wglu_ref[...].astype(BF16))
    y_a = z[:, :d] * jax.nn.sigmoid(z[:, d:])
    y_b = _dot(at_ref[...], wat_ref[...].astype(BF16))
    merged = ga_ref[...].astype(F32) * y_a + gb_ref[...].astype(F32) * y_b
    y = _dot(merged.astype(BF16), wout_ref[...].astype(BF16))
    o_ref[...] = x_ref[...] + gate * _rms(y, gpost_ref[...])


def _merge_call(x, mod_l, g_post, y_ssm, attn, ga, gb, glu_w, attn_w_out, w_out, layer, *, seq, tm):
    t, d = x.shape
    per_b = seq // tm
    row = lambda i: (i, 0)
    ws = [glu_w, attn_w_out, w_out]
    acts = (y_ssm, attn, ga, gb)
    return pl.pallas_call(
        _merge_kernel,
        out_shape=jax.ShapeDtypeStruct((t, d), F32),
        grid=(t // tm,),
        in_specs=[pl.BlockSpec((tm, d), row),
                  pl.BlockSpec((1, 9, d), lambda i: (i // per_b, 0, 0)),
                  _const_spec((1, d))]
                 + [pl.BlockSpec((tm, a.shape[1]), row) for a in acts]
                 + [pl.BlockSpec((None,) + a.shape[1:], lambda i: (layer, 0, 0),
                                 pipeline_mode=pl.Buffered(1)) for a in ws],
        out_specs=pl.BlockSpec((tm, d), row),
        compiler_params=_params(1),
    )(x, mod_l, g_post.reshape(1, d), *acts, *ws)


def kernel(x, c, mod_w, mod_b, norm_pre, norm_post, ffn_w_in, ffn_w_out, mix_w_in, forget_b,
           ssm_a_re, ssm_a_im, ssm_log_dt, ssm_b_re, ssm_b_im, ssm_c_re, ssm_c_im, ssm_d,
           glu_w, attn_w_out, mix_w_out):
    bsz, seq, d = x.shape
    depth = mod_w.shape[0]
    heads = forget_b.shape[1]
    dh = attn_w_out.shape[1] // heads
    groups = ssm_a_re.shape[1]
    ssm_w = ssm_d.shape[1]
    assert heads % 2 == 0 and 2 * dh == LANES and seq % SSM_CHUNK == 0
    tm = min(512, seq)
    tq = min(512, seq)
    tk = tq // 2
    assert seq % tm == 0 and seq % tq == 0

    mod = _mod_call(c, mod_w, mod_b).reshape(depth, bsz, 9, d)
    mats = _s5_prep_call(ssm_a_re, ssm_a_im, ssm_log_dt, ssm_b_re, ssm_b_im, ssm_c_re, ssm_c_im, ssm_d)
    xt = x.reshape(bsz * seq, d)
    for l in range(depth):
        mod_l = mod[l]
        xt = _ffn_call(xt, mod_l, norm_pre[l, 0], norm_post[l, 0], ffn_w_in, ffn_w_out, l, 0,
                       sub=0, seq=seq, tm=tm)
        u, q_aug, k_aug, vt, ga, gb = _inproj_call(xt, mod_l, norm_pre[l, 1], mix_w_in[l], forget_b[l],
                                                   heads=heads, dh=dh, ssm_w=ssm_w, bsz=bsz, seq=seq,
                                                   tm=tm, tk=tk)
        attn = _attn_call(q_aug, k_aug, vt, heads=heads, dh=dh, bsz=bsz, seq=seq, tq=tq, tk=tk)
        y_ssm = _s5_call(u, mats, l, bsz=bsz, seq=seq)
        xt = _merge_call(xt, mod_l, norm_post[l, 1], y_ssm, attn, ga, gb, glu_w, attn_w_out,
                         mix_w_out, l, seq=seq, tm=tm)
        xt = _ffn_call(xt, mod_l, norm_pre[l, 2], norm_post[l, 2], ffn_w_in, ffn_w_out, l, 1,
                       sub=2, seq=seq, tm=tm)
    return xt.reshape(bsz, seq, d)
```

```python
import functools
import math

import jax
import jax.numpy as jnp
from jax import lax
from jax.experimental import pallas as pl
from jax.experimental.pallas import tpu as pltpu

F32 = jnp.float32
BF16 = jnp.bfloat16

RMS_EPS = 1e-6
FFN_RES = 0.5
A_RE_MAX = -1e-4
LOG2E = math.log2(math.e)
LANES = 128
HEAD_PAD = 128
V_EXTRA = 16
SSM_CHUNK = 16
SCAN_ROWS = 16
VMEM_LIMIT = 56 * 1024 * 1024
HIGHEST = lax.Precision.HIGHEST


def _dot(a, b, precision=None):
    return jnp.dot(a, b, preferred_element_type=F32, precision=precision)


def _rms(x, g):
    ms = jnp.mean(x * x, axis=-1, keepdims=True)
    return x * lax.rsqrt(ms + RMS_EPS) * g


def _silu(x):
    return x * jax.nn.sigmoid(x)


def _params(n_axes, semantics="arbitrary"):
    return pltpu.CompilerParams(dimension_semantics=(semantics,) * n_axes,
                                vmem_limit_bytes=VMEM_LIMIT)


def _const_spec(shape):
    zeros = (0,) * len(shape)
    return pl.BlockSpec(shape, lambda *_: zeros, pipeline_mode=pl.Buffered(1))


def _mod_kernel(c_ref, w_ref, b_ref, o_ref):
    sc = _silu(c_ref[...]).astype(BF16)
    o_ref[0] = _dot(sc, w_ref[0].astype(BF16)) + b_ref[0]


def _mod_call(c, mod_w, mod_b):
    depth, d, n = mod_w.shape
    bsz = c.shape[0]
    rows = 8
    c_pad = jnp.zeros((rows, d), F32).at[:bsz].set(c)
    tn = n // 8 if n % (8 * LANES) == 0 else n
    out = pl.pallas_call(
        _mod_kernel,
        out_shape=jax.ShapeDtypeStruct((depth, rows, n), F32),
        grid=(depth, n // tn),
        in_specs=[pl.BlockSpec((rows, d), lambda l, j: (0, 0)),
                  pl.BlockSpec((1, d, tn), lambda l, j: (l, 0, j)),
                  pl.BlockSpec((1, 1, tn), lambda l, j: (l, 0, j))],
        out_specs=pl.BlockSpec((1, rows, tn), lambda l, j: (l, 0, j)),
        compiler_params=_params(2),
    )(c_pad, mod_w, mod_b.reshape(depth, 1, n))
    return out[:, :bsz]


def _mod_rows(mod_ref, sub):
    shift = mod_ref[0, 3 * sub:3 * sub + 1, :]
    scale = mod_ref[0, 3 * sub + 1:3 * sub + 2, :]
    gate = mod_ref[0, 3 * sub + 2:3 * sub + 3, :]
    return shift, scale, gate


def _ffn_kernel(x_ref, mod_ref, gpre_ref, gpost_ref, wi_ref, wo_ref, o_ref, acc_ref, *, sub, tk):
    x = x_ref[...]
    shift, scale, gate = _mod_rows(mod_ref, sub)
    h = (_rms(x, gpre_ref[...]) * (1.0 + scale) + shift).astype(BF16)
    d_ff = wo_ref.shape[0]
    for k in range(d_ff // tk):
        g = _dot(h, wi_ref[:, k * tk:(k + 1) * tk].astype(BF16))
        u = _dot(h, wi_ref[:, d_ff + k * tk:d_ff + (k + 1) * tk].astype(BF16))
        a = (_silu(g) * u).astype(BF16)
        contrib = _dot(a, wo_ref[k * tk:(k + 1) * tk, :].astype(BF16))
        if k == 0:
            acc_ref[...] = contrib
        else:
            acc_ref[...] += contrib
    o_ref[...] = x + (FFN_RES * gate) * _rms(acc_ref[...], gpost_ref[...])


def _ffn_call(x, mod_l, g_pre, g_post, w_in_all, w_out_all, layer, which, *, sub, seq, tm):
    t, d = x.shape
    d_ff = w_out_all.shape[2]
    tk = 256 if d_ff % 256 == 0 else d_ff
    per_b = seq // tm
    row = lambda i: (i, 0)
    pick = lambda i: (layer, which, 0, 0)
    return pl.pallas_call(
        functools.partial(_ffn_kernel, sub=sub, tk=tk),
        out_shape=jax.ShapeDtypeStruct((t, d), F32),
        grid=(t // tm,),
        in_specs=[pl.BlockSpec((tm, d), row),
                  pl.BlockSpec((1, 9, d), lambda i: (i // per_b, 0, 0)),
                  _const_spec((1, d)), _const_spec((1, d)),
                  pl.BlockSpec((None, None, d, 2 * d_ff), pick, pipeline_mode=pl.Buffered(1)),
                  pl.BlockSpec((None, None, d_ff, d), pick, pipeline_mode=pl.Buffered(1))],
        out_specs=pl.BlockSpec((tm, d), row),
        scratch_shapes=[pltpu.VMEM((tm, d), F32)],
        compiler_params=_params(1),
    )(x, mod_l, g_pre.reshape(1, d), g_post.reshape(1, d), w_in_all, w_out_all)


def _split3(c):
    hi = c.astype(BF16).astype(F32)
    r = c - hi
    mid = r.astype(BF16).astype(F32)
    lo = (r - mid).astype(BF16).astype(F32)
    return hi, mid, lo


def _inproj_kernel(x_ref, mod_ref, gpre_ref, fb_ref, wl_ref, wf_ref, wga_ref, wgb_ref,
                   u_ref, q_ref, k_ref, vt_ref, ga_ref, gb_ref, carry_ref,
                   *, heads, dh, tk, ssm_w):
    tm = x_ref.shape[0]
    aw = heads * dh
    w_cols = lambda start, n: wl_ref[:, start:start + n].astype(BF16)

    @pl.when(pl.program_id(1) == 0)
    def _():
        carry_ref[...] = jnp.zeros_like(carry_ref)

    shift, scale, _ = _mod_rows(mod_ref, 1)
    h = (_rms(x_ref[...], gpre_ref[...]) * (1.0 + scale) + shift).astype(BF16)

    lane = lax.broadcasted_iota(jnp.int32, (tm, LANES), 1)
    log_f = jax.nn.log_sigmoid(_dot(h, wf_ref[...]) + fb_ref[...])
    hi, mid, lo = _split3(jnp.where(lane < heads, log_f, 0.0))
    packed = hi + pltpu.roll(mid, heads, axis=1) + pltpu.roll(lo, 2 * heads, axis=1)
    r = lax.broadcasted_iota(jnp.int32, (tm, tm), 0)
    c = lax.broadcasted_iota(jnp.int32, (tm, tm), 1)
    part = _dot((c <= r).astype(BF16), packed.astype(BF16))
    part = part + pltpu.roll(part, LANES - heads, axis=1) + pltpu.roll(part, LANES - 2 * heads, axis=1)
    cum = jnp.where(lane < heads, part, 0.0) + carry_ref[...]
    carry_ref[...] = cum[tm - 1:tm, :]

    q = _dot(h, w_cols(ssm_w, aw)) * (dh ** -0.5 * LOG2E)
    k = _dot(h, w_cols(ssm_w + aw, aw))
    ones_q = jnp.where((lane >= dh) & (lane < dh + 3), 1.0, 0.0)
    ones_k = jnp.where((lane >= dh + 3) & (lane < dh + 6), 1.0, 0.0)
    for hd in range(heads):
        hi, mid, lo = _split3(cum[:, hd:hd + 1] * LOG2E)
        q_aug = jnp.where(lane == dh + 3, hi, jnp.where(lane == dh + 4, mid,
                          jnp.where(lane == dh + 5, lo, ones_q)))
        k_aug = jnp.where(lane == dh, -hi, jnp.where(lane == dh + 1, -mid,
                          jnp.where(lane == dh + 2, -lo, ones_k)))
        src = slice((hd // 2) * LANES, (hd // 2 + 1) * LANES)
        move = (lambda a: a) if hd % 2 == 0 else (lambda a: pltpu.roll(a, dh, axis=1))
        dst = slice(hd * HEAD_PAD, (hd + 1) * HEAD_PAD)
        q_ref[:, dst] = jnp.where(lane < dh, move(q[:, src]), q_aug).astype(BF16)
        k_ref[:, dst] = jnp.where(lane < dh, move(k[:, src]), k_aug).astype(BF16)

    v_t = _dot(h, w_cols(ssm_w + 2 * aw, aw)).T.astype(BF16)
    ones_rows = (lax.broadcasted_iota(jnp.int32, (V_EXTRA, tk), 0) == 0).astype(BF16)
    for hd in range(heads):
        r0 = (hd % 2) * (dh + V_EXTRA)
        for blk in range(tm // tk):
            vt_ref[hd // 2, blk, r0:r0 + dh, :] = v_t[hd * dh:(hd + 1) * dh, blk * tk:(blk + 1) * tk]
            vt_ref[hd // 2, blk, r0 + dh:r0 + dh + V_EXTRA, :] = ones_rows

    u_ref[...] = _dot(h, w_cols(0, ssm_w))
    ga_ref[...] = jax.nn.sigmoid(_dot(h, wga_ref[...])).astype(BF16)
    gb_ref[...] = jax.nn.sigmoid(_dot(h, wgb_ref[...])).astype(BF16)


def _inproj_call(x, mod_l, g_pre, w_in_all, layer, forget_b, *, heads, dh, ssm_w, bsz, seq, tm, tk):
    t, d = x.shape
    aw = heads * dh
    n_lead = ssm_w + 3 * aw
    assert n_lead % LANES == 0
    w = w_in_all[layer]
    wf = jnp.pad(w[:, n_lead:n_lead + heads], ((0, 0), (0, LANES - heads))).astype(BF16)
    wga = w[:, n_lead + heads:n_lead + heads + d].astype(BF16)
    wgb = w[:, n_lead + heads + d:].astype(BF16)
    ws = [wf, wga, wgb]
    fb = jnp.pad(forget_b.reshape(1, heads), ((0, 0), (0, LANES - heads)))
    per_b = seq // tm
    row = lambda b, i: (b * per_b + i, 0)
    hp = heads * HEAD_PAD
    tok = lambda n, dt: (jax.ShapeDtypeStruct((t, n), dt), pl.BlockSpec((tm, n), row))
    vrows = 2 * (dh + V_EXTRA)
    vt = (jax.ShapeDtypeStruct((bsz, heads // 2, seq // tk, vrows, tk), BF16),
          pl.BlockSpec((None, heads // 2, tm // tk, vrows, tk), lambda b, i: (b, 0, i, 0, 0)))
    outs = [tok(ssm_w, F32), tok(hp, BF16), tok(hp, BF16), vt, tok(d, BF16), tok(d, BF16)]
    return pl.pallas_call(
        functools.partial(_inproj_kernel, heads=heads, dh=dh, tk=tk, ssm_w=ssm_w),
        out_shape=[o[0] for o in outs],
        grid=(bsz, per_b),
        in_specs=[pl.BlockSpec((tm, d), row),
                  pl.BlockSpec((1, 9, d), lambda b, i: (b, 0, 0)),
                  _const_spec((1, d)), _const_spec((1, LANES)),
                  pl.BlockSpec((None, d, n_lead), lambda b, i: (layer, 0, 0),
                               pipeline_mode=pl.Buffered(1))]
                 + [_const_spec(a.shape) for a in ws],
        out_specs=[o[1] for o in outs],
        scratch_shapes=[pltpu.VMEM((1, LANES), F32)],
        compiler_params=_params(2),
    )(x, mod_l, g_pre.reshape(1, d), fb, w_in_all, *ws)


def _attn_kernel(q_ref, k_ref, vt_ref, o_ref, s_ref, mx_ref, m_ref, acc_ref, *, tq, tk, dh):
    i = pl.program_id(2)
    assert tq == 2 * tk
    key =lax.broadcasted_iota(jnp.int32, (tk, tq), 0)
    qry = lax.broadcasted_iota(jnp.int32, (tk, tq), 1)
    head_cols = [slice(hh * HEAD_PAD, (hh + 1) * HEAD_PAD) for hh in range(2)]
    qs = [q_ref[:, hs] for hs in head_cols]

    def scores(j, slot, diag):
        rows = pl.ds(pl.multiple_of(j * tk, tk), tk)
        for hh in range(2):
            st = lax.dot_general(k_ref[rows, head_cols[hh]], qs[hh], (((1,), (1,)), ((), ())),
                                 preferred_element_type=F32)
            if diag is not None:
                st = jnp.where(key + diag * tk <= qry, st, -jnp.inf)
            s_ref[slot, hh] = st
            mx_ref[slot, hh] = jnp.max(st, axis=0, keepdims=True)

    vrows = dh + V_EXTRA

    def softmax_pv(j, slot):
        for hh in range(2):
            m = m_ref[hh]
            m_new = jnp.maximum(m, mx_ref[slot, hh])
            alpha = jnp.exp2(m - m_new)
            p = jnp.exp2(s_ref[slot, hh] - m_new)
            acc_ref[hh] = alpha * acc_ref[hh] + _dot(vt_ref[j, hh * vrows:(hh + 1) * vrows, :],
                                                     p.astype(BF16))
            m_ref[hh] = m_new

    d0, d1 = 2 * i, 2 * i + 1
    m_ref[...] = jnp.full(m_ref.shape, -jnp.inf, F32)
    acc_ref[...] = jnp.zeros(acc_ref.shape, F32)
    scores(d0, 0, 0)

    def pair(t):
        scores(2 * t, 1, None)
        softmax_pv(jnp.where(t == 0, d0, 2 * t - 1), 0)
        scores(2 * t + 1, 0, None)
        softmax_pv(2 * t, 1)

    def two_pairs(u, carry):
        pair(2 * u)
        pair(2 * u + 1)
        return carry

    lax.fori_loop(0, lax.shift_right_logical(i, 1), two_pairs, 0)

    @pl.when((i & 1) == 1)
    def _():
        pair(i - 1)

    scores(d1, 1, 1)
    softmax_pv(jnp.where(i == 0, d0, 2 * i - 1), 0)
    softmax_pv(d1, 1)
    o_t = jnp.concatenate([acc_ref[hh, :dh] / acc_ref[hh, dh:dh + 1] for hh in range(2)], axis=0)
    o_ref[...] = o_t.T.astype(BF16)


def _attn_call(q_aug, k_aug, vt, *, heads, dh, bsz, seq, tq, tk):
    t = q_aug.shape[0]
    nq = seq // tq
    nkv = seq // tk
    pair = 2 * HEAD_PAD
    return pl.pallas_call(
        functools.partial(_attn_kernel, tq=tq, tk=tk, dh=dh),
        out_shape=jax.ShapeDtypeStruct((t, heads * dh), BF16),
        grid=(bsz, heads // 2, nq),
        in_specs=[pl.BlockSpec((tq, pair), lambda b, h, i: (b * nq + i, h)),
                  pl.BlockSpec((seq, pair), lambda b, h, i: (b, h)),
                  pl.BlockSpec((None, None, nkv, vt.shape[3], tk), lambda b, h, i: (b, h, 0, 0, 0))],
        out_specs=pl.BlockSpec((tq, 2 * dh), lambda b, h, i: (b * nq + i, h)),
        scratch_shapes=[pltpu.VMEM((2, 2, tk, tq), F32),
                        pltpu.VMEM((2, 2, 1, tq), F32),
                        pltpu.VMEM((2, 1, tq), F32),
                        pltpu.VMEM((2, dh + V_EXTRA, tq), F32)],
        compiler_params=_params(3),
    )(q_aug, k_aug, vt)


def _int_power(re, im, e, e_max):
    pr, pi = jnp.ones(e.shape, F32), jnp.zeros(e.shape, F32)
    bit = 0
    while (1 << bit) <= e_max:
        on = (lax.shift_right_logical(e, bit) & 1) == 1
        pr, pi = jnp.where(on, pr * re - pi * im, pr), jnp.where(on, pr * im + pi * re, pi)
        re, im = re * re - im * im, 2.0 * re * im
        bit += 1
    return pr, pi


def _s5_prep_kernel(re_c_ref, im_c_ref, re_r_ref, im_r_ref, ldt_ref, ct_re_ref, ct_im_ref,
                    bt_re_ref, bt_im_ref, d_ref,
                    mst_ref, ws_ref, wo_ref, a1_ref, a2_ref, b_scr, *, n_ch, p_st):
    cn = SSM_CHUNK * n_ch
    p2 = 2 * p_st
    gpb = LANES // n_ch
    assert cn == 2 * LANES and p2 == LANES
    shift_bits = n_ch.bit_length() - 1
    tau = lax.shift_right_logical(lax.broadcasted_iota(jnp.int32, (p2, cn), 1), shift_bits)
    top = lax.broadcasted_iota(jnp.int32, (p2, cn), 0) < p_st
    lane_b = lax.broadcasted_iota(jnp.int32, (n_ch, p2), 1) < p_st
    zr = lax.broadcasted_iota(jnp.int32, (n_ch, cn), 0)
    zl = lax.broadcasted_iota(jnp.int32, (n_ch, cn), 1)
    srow = lax.shift_right_logical(lax.broadcasted_iota(jnp.int32, (cn, p2), 0), shift_bits)
    lane_s = lax.broadcasted_iota(jnp.int32, (cn, p2), 1) < p_st
    lane_a = lax.broadcasted_iota(jnp.int32, (1, p2), 1) < p_st
    grp_z = lax.shift_right_logical(lax.broadcasted_iota(jnp.int32, (n_ch, LANES), 1), shift_bits)
    grp_r = lax.shift_right_logical(lax.broadcasted_iota(jnp.int32, (p2, LANES), 1), shift_bits)
    spread = (lax.broadcasted_iota(jnp.int32, (n_ch, cn), 0)
              == (lax.broadcasted_iota(jnp.int32, (n_ch, cn), 1) & (n_ch - 1))).astype(F32)

    ws_ref[0] = jnp.zeros(ws_ref.shape[1:], BF16)
    for g in range(gpb):
        dt = jnp.exp(ldt_ref[g])
        are_c = jnp.minimum(re_c_ref[g], A_RE_MAX) * dt
        aim_c = im_c_ref[g] * dt
        re_r = jnp.minimum(re_r_ref[g], A_RE_MAX)
        im_r = im_r_ref[g]
        are_r = re_r * dt
        aim_r = im_r * dt

        mag = jnp.exp(are_r)
        lb_re = mag * jnp.cos(aim_r)
        lb_im = mag * jnp.sin(aim_r)
        den = re_r * re_r + im_r * im_r
        xr = lb_re - 1.0
        coef_re = (xr * re_r + lb_im * im_r) / den
        coef_im = (lb_im * re_r - xr * im_r) / den

        ct_re = _dot(ct_re_ref[g], spread, HIGHEST)
        ct_im = _dot(ct_im_ref[g], spread, HIGHEST)

        mag_c = jnp.exp(are_c)
        lam_c = (mag_c * jnp.cos(aim_c), mag_c * jnp.sin(aim_c))
        pw_re, pw_im = _int_power(lam_c[0], lam_c[1], tau, SSM_CHUNK - 1)

        def times_c(pr, pi):
            rr = pr * ct_re - pi * ct_im
            ri = pr * ct_im + pi * ct_re
            return jnp.where(top, rr, -ri)

        r1 = times_c(pw_re * lam_c[0] - pw_im * lam_c[1], pw_re * lam_c[1] + pw_im * lam_c[0])
        bb_re = coef_re * bt_re_ref[g] - coef_im * bt_im_ref[g]
        bb_im = coef_re * bt_im_ref[g] + coef_im * bt_re_ref[g]
        z = _dot(jnp.where(lane_b, bb_re, bb_im), times_c(pw_re, pw_im), HIGHEST)
        z = z + jnp.where(zr == zl, d_ref[g], 0.0)

        pr, pi = _int_power(lb_re, lb_im, SSM_CHUNK - 1 - srow, SSM_CHUNK - 1)
        tb_re = jnp.concatenate([bb_re] * SSM_CHUNK, axis=0)
        tb_im = jnp.concatenate([bb_im] * SSM_CHUNK, axis=0)
        ws_g =jnp.where(lane_s, pr * tb_re - pi * tb_im, pr * tb_im + pi * tb_re).astype(BF16)

        for t in range(SSM_CHUNK):
            half = slice((t // gpb) * LANES, (t // gpb + 1) * LANES)
            shift = ((g - t % gpb) * n_ch) % LANES
            move = (lambda a: a) if shift == 0 else (lambda a: pltpu.roll(a, shift, axis=1))
            b_scr[t, g * n_ch:(g + 1) * n_ch, :] = jnp.where(grp_z == g, move(z[:, half]), 0.0)
            wo_ref[0, g * p2:(g + 1) * p2, t * LANES:(t + 1) * LANES] = (
                jnp.where(grp_r == g, move(r1[:, half]), 0.0).astype(BF16))
            ws_ref[0, t * LANES + g * n_ch:t * LANES + (g + 1) * n_ch, g * p2:(g + 1) * p2] = (
                ws_g[t * n_ch:(t + 1) * n_ch, :])

        ar, ai = lb_re, lb_im
        for _ in range(SSM_CHUNK.bit_length() - 1):
            ar, ai = ar * ar - ai * ai, 2.0 * ar * ai
        for k in range(SCAN_ROWS):
            a1_ref[0, k:k + 1, g * p2:(g + 1) * p2] = ar
            a2_ref[0, k:k + 1, g * p2:(g + 1) * p2] = jnp.where(lane_a, -ai, ai)
            ar, ai = ar * ar - ai * ai, 2.0 * ar * ai

    pairs = SSM_CHUNK // 2
    for d in range(pairs):
        r0 = (pairs - 1 - d) * 2 * LANES
        diag = b_scr[2 * d].astype(BF16)
        below = b_scr[2 * d - 1].astype(BF16) if d > 0 else jnp.zeros((LANES, LANES), BF16)
        mst_ref[0, r0:r0 + LANES, :LANES] = diag
        mst_ref[0, r0:r0 + LANES, LANES:] = b_scr[2 * d + 1].astype(BF16)
        mst_ref[0, r0 + LANES:r0 + 2 * LANES, :LANES] = below
        mst_ref[0, r0 + LANES:r0 + 2 * LANES, LANES:] = diag


def _s5_prep_call(a_re, a_im, log_dt, b_re, b_im, c_re, c_im, d_skip):
    p_st, n_ch = a_re.shape[-1], b_re.shape[-1]
    flat = lambda a, k: a.reshape((-1,) + a.shape[a.ndim - k:])
    a_re, a_im, log_dt = flat(a_re, 1), flat(a_im, 1), flat(log_dt, 0)
    b_re, b_im, c_re, c_im = flat(b_re, 2), flat(b_im, 2), flat(c_re, 2), flat(c_im, 2)
    g = a_re.shape[0]
    cn = SSM_CHUNK * n_ch
    p2 = 2 * p_st
    gpb = LANES // n_ch
    nblk = g // gpb
    assert g % gpb == 0
    col = lambda a: jnp.tile(a[:, :, None], (1, 2, 1))
    rowv = lambda a: jnp.tile(a[:, None, :], (1, 1, 2))
    ct = lambda a: jnp.tile(a.transpose(0, 2, 1), (1, 2, 1))
    bt = lambda a: jnp.tile(a.transpose(0, 2, 1), (1, 1, 2))
    d_pad = jnp.pad(d_skip.reshape(g, 1, n_ch), ((0, 0), (0, 0), (0, cn - n_ch)))
    args = (col(a_re), col(a_im), rowv(a_re), rowv(a_im), log_dt.reshape(g, 1, 1),
            ct(c_re), ct(c_im), bt(b_re), bt(b_im), d_pad)
    out_shapes = [jax.ShapeDtypeStruct((nblk, SSM_CHUNK * LANES, 2 * LANES), BF16),
                  jax.ShapeDtypeStruct((nblk, SSM_CHUNK * LANES, gpb * p2), BF16),
                  jax.ShapeDtypeStruct((nblk, gpb * p2, SSM_CHUNK * LANES), BF16),
                  jax.ShapeDtypeStruct((nblk, SCAN_ROWS, gpb * p2), F32),
                  jax.ShapeDtypeStruct((nblk, SCAN_ROWS, gpb * p2), F32)]
    return pl.pallas_call(
        functools.partial(_s5_prep_kernel, n_ch=n_ch, p_st=p_st),
        out_shape=out_shapes,
        grid=(nblk,),
        in_specs=[pl.BlockSpec((gpb,) + a.shape[1:], lambda i: (i, 0, 0)) for a in args],
        out_specs=[pl.BlockSpec((1,) + s.shape[1:], lambda i: (i, 0, 0)) for s in out_shapes],
        scratch_shapes=[pltpu.VMEM((SSM_CHUNK, LANES, LANES), F32)],
        compiler_params=_params(1),
    )(*args)


def _s5_kernel(u_ref, mst_ref, ws_ref, wo_ref, a1_ref, a2_ref, y_ref, xx_ref, sin_ref, *, p_st):
    ncb = xx_ref.shape[0]
    p2 = 2 * p_st
    pair = 2 * LANES
    pairs = SSM_CHUNK // 2
    for t in range(SSM_CHUNK):
        xx_ref[:, t * LANES:(t + 1) * LANES] = u_ref[pl.ds(t, ncb, stride=SSM_CHUNK), :].astype(BF16)
    s_all = _dot(xx_ref[...], ws_ref[0])
    row = lax.broadcasted_iota(jnp.int32, (ncb, p2), 0)
    for g in range(s_all.shape[1] // p2):
        cols = slice(g * p2, (g + 1) * p2)
        x = s_all[:, cols]
        d, k = 1, 0
        while d < ncb:
            xs = jnp.where(row >= d, pltpu.roll(x, d, axis=0), 0.0)
            x = x + a1_ref[0, k:k + 1, cols] * xs + a2_ref[0, k:k + 1, cols] * pltpu.roll(xs, p_st, axis=1)
            d, k = 2 * d, k + 1
        sin_ref[:, cols] = jnp.where(row >= 1, pltpu.roll(x, 1, axis=0), 0.0).astype(BF16)
    for tp in range(pairs):
        yy = (_dot(xx_ref[:, :pair * (tp + 1)], mst_ref[0, (pairs - 1 - tp) * pair:, :])
              + _dot(sin_ref[...], wo_ref[0, :, tp * pair:(tp + 1) * pair]))
        y_ref[pl.ds(2 * tp, ncb, stride=SSM_CHUNK), :] = yy[:, :LANES]
        y_ref[pl.ds(2 * tp + 1, ncb, stride=SSM_CHUNK), :] = yy[:, LANES:]


def _s5_call(u, mats, layer, *, bsz, seq):
    mst, ws, wo, a1, a2 = mats
    t, ssm_w = u.shape
    nblk = ssm_w // LANES
    ncb = seq // SSM_CHUNK
    p_st = LANES // 2
    assert ncb < 2 ** SCAN_ROWS
    wspec = lambda a: pl.BlockSpec((1,) + a.shape[1:], lambda o, b: (layer * nblk + o, 0, 0))
    return pl.pallas_call(
        functools.partial(_s5_kernel, p_st=p_st),
        out_shape=jax.ShapeDtypeStruct((t, ssm_w), F32),
        grid=(nblk, bsz),
        in_specs=[pl.BlockSpec((seq, LANES), lambda o, b: (b, o))] + [wspec(a) for a in mats],
        out_specs=pl.BlockSpec((seq, LANES), lambda o, b: (b, o)),
        scratch_shapes=[pltpu.VMEM((ncb, SSM_CHUNK * LANES), BF16),
                        pltpu.VMEM((ncb, ws.shape[2]), BF16)],
        compiler_params=_params(2),
    )(u, *mats)


def _merge_kernel(x_ref, mod_ref, gpost_ref, ys_ref, at_ref, ga_ref, gb_ref,
                  wglu_ref, wat_ref, wout_ref, o_ref):
    d = x_ref.shape[1]
    _, _, gate = _mod_rows(mod_ref, 1)
    z = _dot(jax.nn.gelu(ys_ref[...]).astype(BF16), wglu_ref[...].astype(BF16))
    y_a = z[:, :d] * jax.nn.sigmoid(z[:, d:])
    y_b = _dot(at_ref[...], wat_ref[...].astype(BF16))
    merged = ga_ref[...].astype(F32) * y_a + gb_ref[...].astype(F32) * y_b
    y = _dot(merged.astype(BF16), wout_ref[...].astype(BF16))
    o_ref[...] = x_ref[...] + gate * _rms(y, gpost_ref[...])


def _merge_call(x, mod_l, g_post, y_ssm, attn, ga, gb, glu_w, attn_w_out, w_out, layer, *, seq, tm):
    t, d = x.shape
    per_b = seq // tm
    row = lambda i: (i, 0)
    ws = [glu_w, attn_w_out, w_out]
    acts = (y_ssm, attn, ga, gb)
    return pl.pallas_call(
        _merge_kernel,
        out_shape=jax.ShapeDtypeStruct((t, d), F32),
        grid=(t // tm,),
        in_specs=[pl.BlockSpec((tm, d), row),
                  pl.BlockSpec((1, 9, d), lambda i: (i // per_b, 0, 0)),
                  _const_spec((1, d))]
                 + [pl.BlockSpec((tm, a.shape[1]), row) for a in acts]
                 + [pl.BlockSpec((None,) + a.shape[1:], lambda i: (layer, 0, 0),
                                 pipeline_mode=pl.Buffered(1)) for a in ws],
        out_specs=pl.BlockSpec((tm, d), row),
        compiler_params=_params(1),
    )(x, mod_l, g_post.reshape(1, d), *acts, *ws)


def kernel(x, c, mod_w, mod_b, norm_pre, norm_post, ffn_w_in, ffn_w_out, mix_w_in, forget_b,
           ssm_a_re, ssm_a_im, ssm_log_dt, ssm_b_re, ssm_b_im, ssm_c_re, ssm_c_im, ssm_d,
           glu_w, attn_w_out, mix_w_out):
    bsz, seq, d = x.shape
    depth = mod_w.shape[0]
    heads = forget_b.shape[1]
    dh = attn_w_out.shape[1] // heads
    groups = ssm_a_re.shape[1]
    ssm_w = ssm_d.shape[1]
    assert heads % 2 == 0 and 2 * dh == LANES and seq % SSM_CHUNK == 0
    tm = min(512, seq)
    tq = min(1024, seq)
    tk = tq // 2
    assert seq % tm == 0 and seq % tq == 0

    mod = _mod_call(c, mod_w, mod_b).reshape(depth, bsz, 9, d)
    mats = _s5_prep_call(ssm_a_re, ssm_a_im, ssm_log_dt, ssm_b_re, ssm_b_im, ssm_c_re, ssm_c_im, ssm_d)
    xt = x.reshape(bsz * seq, d)
    for l in range(depth):
        mod_l = mod[l]
        xt = _ffn_call(xt, mod_l, norm_pre[l, 0], norm_post[l, 0], ffn_w_in, ffn_w_out, l, 0,
                       sub=0, seq=seq, tm=tm)
        u, q_aug, k_aug, vt, ga, gb = _inproj_call(xt, mod_l, norm_pre[l, 1], mix_w_in, l, forget_b[l],
                                                   heads=heads, dh=dh, ssm_w=ssm_w, bsz=bsz, seq=seq,
                                                   tm=tm, tk=tk)
        attn = _attn_call(q_aug, k_aug, vt, heads=heads, dh=dh, bsz=bsz, seq=seq, tq=tq, tk=tk)
        y_ssm = _s5_call(u, mats, l, bsz=bsz, seq=seq)
        xt = _merge_call(xt, mod_l, norm_post[l, 1], y_ssm, attn, ga, gb, glu_w, attn_w_out,
                         mix_w_out, l, seq=seq, tm=tm)
        xt = _ffn_call(xt, mod_l, norm_pre[l, 2], norm_post[l, 2], ffn_w_in, ffn_w_out, l, 1,
                       sub=2, seq=seq, tm=tm)
    return xt.reshape(bsz, seq, d)
```

```python
import functools
import math

import jax
import jax.numpy as jnp
from jax import lax
from jax.experimental import pallas as pl
from jax.experimental.pallas import tpu as pltpu

F32 = jnp.float32
BF16 = jnp.bfloat16

RMS_EPS = 1e-6
FFN_RES = 0.5
A_RE_MAX = -1e-4
LOG2E = math.log2(math.e)
LANES = 128
HEAD_PAD = 128
V_EXTRA = 16
SSM_CHUNK = 16
SCAN_ROWS = 16
VMEM_LIMIT = 56 * 1024 * 1024
HIGHEST = lax.Precision.HIGHEST


def _dot(a, b, precision=None):
    return jnp.dot(a, b, preferred_element_type=F32, precision=precision)


def _rms(x, g):
    ms = jnp.mean(x * x, axis=-1, keepdims=True)
    return x * lax.rsqrt(ms + RMS_EPS) * g


def _silu(x):
    return x * jax.nn.sigmoid(x)


def _params(n_axes, semantics="arbitrary"):
    return pltpu.CompilerParams(dimension_semantics=(semantics,) * n_axes,
                                vmem_limit_bytes=VMEM_LIMIT)


def _const_spec(shape):
    zeros = (0,) * len(shape)
    return pl.BlockSpec(shape, lambda *_: zeros, pipeline_mode=pl.Buffered(1))


def _mod_kernel(c_ref, w_ref, b_ref, o_ref):
    sc = _silu(c_ref[...]).astype(BF16)
    o_ref[0] = _dot(sc, w_ref[0].astype(BF16)) + b_ref[0]


def _mod_call(c, mod_w, mod_b):
    depth, d, n = mod_w.shape
    bsz = c.shape[0]
    rows = 8
    c_pad = jnp.zeros((rows, d), F32).at[:bsz].set(c)
    tn = n // 8 if n % (8 * LANES) == 0 else n
    out = pl.pallas_call(
        _mod_kernel,
        out_shape=jax.ShapeDtypeStruct((depth, rows, n), F32),
        grid=(depth, n // tn),
        in_specs=[pl.BlockSpec((rows, d), lambda l, j: (0, 0)),
                  pl.BlockSpec((1, d, tn), lambda l, j: (l, 0, j)),
                  pl.BlockSpec((1, 1, tn), lambda l, j: (l, 0, j))],
        out_specs=pl.BlockSpec((1, rows, tn), lambda l, j: (l, 0, j)),
        compiler_params=_params(2),
    )(c_pad, mod_w, mod_b.reshape(depth, 1, n))
    return out[:, :bsz]


def _mod_rows(mod_ref, sub):
    shift = mod_ref[0, 3 * sub:3 * sub + 1, :]
    scale = mod_ref[0, 3 * sub + 1:3 * sub + 2, :]
    gate = mod_ref[0, 3 * sub + 2:3 * sub + 3, :]
    return shift, scale, gate


def _ffn_kernel(x_ref, mod_ref, gpre_ref, gpost_ref, wi_ref, wo_ref, o_ref, acc_ref, *, sub, tk):
    x = x_ref[...]
    shift, scale, gate = _mod_rows(mod_ref, sub)
    h = (_rms(x, gpre_ref[...]) * (1.0 + scale) + shift).astype(BF16)
    d_ff = wo_ref.shape[0]
    for k in range(d_ff // tk):
        g = _dot(h, wi_ref[:, k * tk:(k + 1) * tk].astype(BF16))
        u = _dot(h, wi_ref[:, d_ff + k * tk:d_ff + (k + 1) * tk].astype(BF16))
        a = (_silu(g) * u).astype(BF16)
        contrib = _dot(a, wo_ref[k * tk:(k + 1) * tk, :].astype(BF16))
        if k == 0:
            acc_ref[...] = contrib
        else:
            acc_ref[...] += contrib
    o_ref[...] = x + (FFN_RES * gate) * _rms(acc_ref[...], gpost_ref[...])


def _ffn_call(x, mod_l, g_pre, g_post, w_in_all, w_out_all, layer, which, *, sub, seq, tm):
    t, d = x.shape
    d_ff = w_out_all.shape[2]
    tk = 256 if d_ff % 256 == 0 else d_ff
    per_b = seq // tm
    row = lambda i: (i, 0)
    pick = lambda i: (layer, which, 0, 0)
    return pl.pallas_call(
        functools.partial(_ffn_kernel, sub=sub, tk=tk),
        out_shape=jax.ShapeDtypeStruct((t, d), F32),
        grid=(t // tm,),
        in_specs=[pl.BlockSpec((tm, d), row),
                  pl.BlockSpec((1, 9, d), lambda i: (i // per_b, 0, 0)),
                  _const_spec((1, d)), _const_spec((1, d)),
                  pl.BlockSpec((None, None, d, 2 * d_ff), pick, pipeline_mode=pl.Buffered(1)),
                  pl.BlockSpec((None, None, d_ff, d), pick, pipeline_mode=pl.Buffered(1))],
        out_specs=pl.BlockSpec((tm, d), row),
        scratch_shapes=[pltpu.VMEM((tm, d), F32)],
        compiler_params=_params(1),
    )(x, mod_l, g_pre.reshape(1, d), g_post.reshape(1, d), w_in_all, w_out_all)


def _split3(c):
    hi = c.astype(BF16).astype(F32)
    r = c - hi
    mid = r.astype(BF16).astype(F32)
    lo = (r - mid).astype(BF16).astype(F32)
    return hi, mid, lo


def _inproj_kernel(x_ref, mod_ref, gpre_ref, fb_ref, wl_ref, wf_ref, wga_ref, wgb_ref,
                   u_ref, q_ref, k_ref, vt_ref, ga_ref, gb_ref, carry_ref,
                   *, heads, dh, tk, ssm_w):
    tm = x_ref.shape[0]
    aw = heads * dh
    w_cols = lambda start, n: wl_ref[:, start:start + n].astype(BF16)

    @pl.when(pl.program_id(1) == 0)
    def _():
        carry_ref[...] = jnp.zeros_like(carry_ref)

    shift, scale, _ = _mod_rows(mod_ref, 1)
    h = (_rms(x_ref[...], gpre_ref[...]) * (1.0 + scale) + shift).astype(BF16)

    lane = lax.broadcasted_iota(jnp.int32, (tm, LANES), 1)
    log_f = jax.nn.log_sigmoid(_dot(h, wf_ref[...]) + fb_ref[...])
    hi, mid, lo = _split3(jnp.where(lane < heads, log_f, 0.0))
    packed = hi + pltpu.roll(mid, heads, axis=1) + pltpu.roll(lo, 2 * heads, axis=1)
    r = lax.broadcasted_iota(jnp.int32, (tm, tm), 0)
    c = lax.broadcasted_iota(jnp.int32, (tm, tm), 1)
    part = _dot((c <= r).astype(BF16), packed.astype(BF16))
    part = part + pltpu.roll(part, LANES - heads, axis=1) + pltpu.roll(part, LANES - 2 * heads, axis=1)
    cum = jnp.where(lane < heads, part, 0.0) + carry_ref[...]
    carry_ref[...] = cum[tm - 1:tm, :]

    q = _dot(h, w_cols(ssm_w, aw)) * (dh ** -0.5 * LOG2E)
    k = _dot(h, w_cols(ssm_w + aw, aw))
    ones_q = jnp.where((lane >= dh) & (lane < dh + 3), 1.0, 0.0)
    ones_k = jnp.where((lane >= dh + 3) & (lane < dh + 6), 1.0, 0.0)
    for hd in range(heads):
        hi, mid, lo = _split3(cum[:, hd:hd + 1] * LOG2E)
        q_aug = jnp.where(lane == dh + 3, hi, jnp.where(lane == dh + 4, mid,
                          jnp.where(lane == dh + 5, lo, ones_q)))
        k_aug = jnp.where(lane == dh, -hi, jnp.where(lane == dh + 1, -mid,
                          jnp.where(lane == dh + 2, -lo, ones_k)))
        src = slice((hd // 2) * LANES, (hd // 2 + 1) * LANES)
        move = (lambda a: a) if hd % 2 == 0 else (lambda a: pltpu.roll(a, dh, axis=1))
        dst = slice(hd * HEAD_PAD, (hd + 1) * HEAD_PAD)
        q_ref[:, dst] = jnp.where(lane < dh, move(q[:, src]), q_aug).astype(BF16)
        k_ref[:, dst] = jnp.where(lane < dh, move(k[:, src]), k_aug).astype(BF16)

    v_t = _dot(h, w_cols(ssm_w + 2 * aw, aw)).T.astype(BF16)
    ones_rows = (lax.broadcasted_iota(jnp.int32, (V_EXTRA, tk), 0) == 0).astype(BF16)
    for hd in range(heads):
        r0 = (hd % 2) * (dh + V_EXTRA)
        for blk in range(tm // tk):
            vt_ref[hd // 2, blk, r0:r0 + dh, :] = v_t[hd * dh:(hd + 1) * dh, blk * tk:(blk + 1) * tk]
            vt_ref[hd // 2, blk, r0 + dh:r0 + dh + V_EXTRA, :] = ones_rows

    u_ref[...] = _dot(h, w_cols(0, ssm_w))
    ga_ref[...] = jax.nn.sigmoid(_dot(h, wga_ref[...])).astype(BF16)
    gb_ref[...] = jax.nn.sigmoid(_dot(h, wgb_ref[...])).astype(BF16)


def _inproj_call(x, mod_l, g_pre, w_in_all, layer, forget_b, *, heads, dh, ssm_w, bsz, seq, tm, tk):
    t, d = x.shape
    aw = heads * dh
    n_lead = ssm_w + 3 * aw
    assert n_lead % LANES == 0
    w = w_in_all[layer]
    wf = jnp.pad(w[:, n_lead:n_lead + heads], ((0, 0), (0, LANES - heads))).astype(BF16)
    wga = w[:, n_lead + heads:n_lead + heads + d].astype(BF16)
    wgb = w[:, n_lead + heads + d:].astype(BF16)
    ws = [wf, wga, wgb]
    fb = jnp.pad(forget_b.reshape(1, heads), ((0, 0), (0, LANES - heads)))
    per_b = seq // tm
    row = lambda b, i: (b * per_b + i, 0)
    hp = heads * HEAD_PAD
    tok = lambda n, dt: (jax.ShapeDtypeStruct((t, n), dt), pl.BlockSpec((tm, n), row))
    vrows = 2 * (dh + V_EXTRA)
    vt = (jax.ShapeDtypeStruct((bsz, heads // 2, seq // tk, vrows, tk), BF16),
          pl.BlockSpec((None, heads // 2, tm // tk, vrows, tk), lambda b, i: (b, 0, i, 0, 0)))
    outs = [tok(ssm_w, F32), tok(hp, BF16), tok(hp, BF16), vt, tok(d, BF16), tok(d, BF16)]
    return pl.pallas_call(
        functools.partial(_inproj_kernel, heads=heads, dh=dh, tk=tk, ssm_w=ssm_w),
        out_shape=[o[0] for o in outs],
        grid=(bsz, per_b),
        in_specs=[pl.BlockSpec((tm, d), row),
                  pl.BlockSpec((1, 9, d), lambda b, i: (b, 0, 0)),
                  _const_spec((1, d)), _const_spec((1, LANES)),
                  _const_spec((d, n_lead))] + [_const_spec(a.shape) for a in ws],
        out_specs=[o[1] for o in outs],
        scratch_shapes=[pltpu.VMEM((1, LANES), F32)],
        compiler_params=_params(2),
    )(x, mod_l, g_pre.reshape(1, d), fb, w[:, :n_lead], *ws)


def _attn_kernel(q_ref, k_ref, vt_ref, o_ref, s_ref, mx_ref, m_ref, acc_ref, *, tq, tk, dh):
    i = pl.program_id(2)
    assert tq == 2 * tk
    visible = (lax.broadcasted_iota(jnp.int32, (tk, tk), 0)
               <= lax.broadcasted_iota(jnp.int32, (tk, tk), 1))
    head_cols = [slice(hh * HEAD_PAD, (hh + 1) * HEAD_PAD) for hh in range(2)]
    both, late = (0, 1), (1,)

    def scores(j, slot, halves=both, masked_half=None):
        rows = pl.ds(pl.multiple_of(j * tk, tk), tk)
        for hh in range(2):
            for hf in halves:
                q = q_ref[hf * tk:(hf + 1) * tk, head_cols[hh]]
                st = lax.dot_general(k_ref[rows, head_cols[hh]], q, (((1,), (1,)), ((), ())),
                                     preferred_element_type=F32)
                if hf == masked_half:
                    st = jnp.where(visible, st, -jnp.inf)
                s_ref[slot, hh, hf] = st
                mx_ref[slot, hh, hf] = jnp.max(st, axis=0, keepdims=True)

    vrows = dh + V_EXTRA

    def softmax_pv(j, slot, halves=both):
        for hh in range(2):
            for hf in halves:
                m = m_ref[hh, hf]
                m_new = jnp.maximum(m, mx_ref[slot, hh, hf])
                m_ref[hh, hf] = m_new
                p = jnp.exp2(s_ref[slot, hh, hf] - m_new)
                acc_ref[hh, hf] = jnp.exp2(m - m_new) * acc_ref[hh, hf] + _dot(
                    vt_ref[j, hh * vrows:(hh + 1) * vrows, :], p.astype(BF16))

    d0, d1 = 2 * i, 2 * i + 1
    m_ref[...] = jnp.full(m_ref.shape, -jnp.inf, F32)
    acc_ref[...] = jnp.zeros(acc_ref.shape, F32)
    scores(d0, 0, masked_half=0)

    def pair(t):
        scores(2 * t, 1)
        softmax_pv(jnp.where(t == 0, d0, 2 * t - 1), 0)
        scores(2 * t + 1, 0)
        softmax_pv(2 * t, 1)

    def two_pairs(u, carry):
        pair(2 * u)
        pair(2 * u + 1)
        return carry

    lax.fori_loop(0, lax.shift_right_logical(i, 1), two_pairs, 0)

    @pl.when((i & 1) == 1)
    def _():
        pair(i - 1)

    scores(d1, 1, late, masked_half=1)
    softmax_pv(jnp.where(i == 0, d0, 2 * i - 1), 0)
    softmax_pv(d1, 1, late)
    for hf in both:
        o_t = jnp.concatenate([acc_ref[hh, hf, :dh] / acc_ref[hh, hf, dh:dh + 1] for hh in range(2)],
                              axis=0)
        o_ref[hf * tk:(hf + 1) * tk, :] = o_t.T.astype(BF16)


def _attn_call(q_aug, k_aug, vt, *, heads, dh, bsz, seq, tq, tk):
    t = q_aug.shape[0]
    nq = seq // tq
    nkv = seq // tk
    pair = 2 * HEAD_PAD
    return pl.pallas_call(
        functools.partial(_attn_kernel, tq=tq, tk=tk, dh=dh),
        out_shape=jax.ShapeDtypeStruct((t, heads * dh), BF16),
        grid=(bsz, heads // 2, nq),
        in_specs=[pl.BlockSpec((tq, pair), lambda b, h, i: (b * nq + i, h)),
                  pl.BlockSpec((seq, pair), lambda b, h, i: (b, h)),
                  pl.BlockSpec((None, None, nkv, vt.shape[3], tk), lambda b, h, i: (b, h, 0, 0, 0))],
        out_specs=pl.BlockSpec((tq, 2 * dh), lambda b, h, i: (b * nq + i, h)),
        scratch_shapes=[pltpu.VMEM((2, 2, 2, tk, tk), F32),
                        pltpu.VMEM((2, 2, 2, 1, tk), F32),
                        pltpu.VMEM((2, 2, 1, tk), F32),
                        pltpu.VMEM((2, 2, dh + V_EXTRA, tk), F32)],
        compiler_params=_params(3),
    )(q_aug, k_aug, vt)


def _int_power(re, im, e, e_max):
    pr, pi = jnp.ones(e.shape, F32), jnp.zeros(e.shape, F32)
    bit = 0
    while (1 << bit) <= e_max:
        on = (lax.shift_right_logical(e, bit) & 1) == 1
        pr, pi = jnp.where(on, pr * re - pi * im, pr), jnp.where(on, pr * im + pi * re, pi)
        re, im = re * re - im * im, 2.0 * re * im
        bit += 1
    return pr, pi


def _s5_prep_kernel(re_c_ref, im_c_ref, re_r_ref, im_r_ref, ldt_ref, ct_re_ref, ct_im_ref,
                    bt_re_ref, bt_im_ref, d_ref,
                    mst_ref, ws_ref, wo_ref, a1_ref, a2_ref, b_scr, *, n_ch, p_st):
    cn = SSM_CHUNK * n_ch
    p2 = 2 * p_st
    gpb = LANES // n_ch
    assert cn == 2 * LANES and p2 == LANES
    shift_bits = n_ch.bit_length() - 1
    tau = lax.shift_right_logical(lax.broadcasted_iota(jnp.int32, (p2, cn), 1), shift_bits)
    top = lax.broadcasted_iota(jnp.int32, (p2, cn), 0) < p_st
    lane_b = lax.broadcasted_iota(jnp.int32, (n_ch, p2), 1) < p_st
    zr = lax.broadcasted_iota(jnp.int32, (n_ch, cn), 0)
    zl = lax.broadcasted_iota(jnp.int32, (n_ch, cn), 1)
    srow = lax.shift_right_logical(lax.broadcasted_iota(jnp.int32, (cn, p2), 0), shift_bits)
    lane_s = lax.broadcasted_iota(jnp.int32, (cn, p2), 1) < p_st
    lane_a = lax.broadcasted_iota(jnp.int32, (1, p2), 1) < p_st
    grp_z = lax.shift_right_logical(lax.broadcasted_iota(jnp.int32, (n_ch, LANES), 1), shift_bits)
    grp_r = lax.shift_right_logical(lax.broadcasted_iota(jnp.int32, (p2, LANES), 1), shift_bits)
    spread = (lax.broadcasted_iota(jnp.int32, (n_ch, cn), 0)
              == (lax.broadcasted_iota(jnp.int32, (n_ch, cn), 1) & (n_ch - 1))).astype(F32)

    ws_ref[0] = jnp.zeros(ws_ref.shape[1:], BF16)
    for g in range(gpb):
        dt = jnp.exp(ldt_ref[g])
        are_c = jnp.minimum(re_c_ref[g], A_RE_MAX) * dt
        aim_c = im_c_ref[g] * dt
        re_r = jnp.minimum(re_r_ref[g], A_RE_MAX)
        im_r = im_r_ref[g]
        are_r = re_r * dt
        aim_r = im_r * dt

        mag = jnp.exp(are_r)
        lb_re = mag * jnp.cos(aim_r)
        lb_im = mag * jnp.sin(aim_r)
        den = re_r * re_r + im_r * im_r
        xr = lb_re - 1.0
        coef_re = (xr * re_r + lb_im * im_r) / den
        coef_im = (lb_im * re_r - xr * im_r) / den

        ct_re = _dot(ct_re_ref[g], spread, HIGHEST)
        ct_im = _dot(ct_im_ref[g], spread, HIGHEST)

        mag_c = jnp.exp(are_c)
        lam_c = (mag_c * jnp.cos(aim_c), mag_c * jnp.sin(aim_c))
        pw_re, pw_im = _int_power(lam_c[0], lam_c[1], tau, SSM_CHUNK - 1)

        def times_c(pr, pi):
            rr = pr * ct_re - pi * ct_im
            ri = pr * ct_im + pi * ct_re
            return jnp.where(top, rr, -ri)

        r1 = times_c(pw_re * lam_c[0] - pw_im * lam_c[1], pw_re * lam_c[1] + pw_im * lam_c[0])
        bb_re = coef_re * bt_re_ref[g] - coef_im * bt_im_ref[g]
        bb_im = coef_re * bt_im_ref[g] + coef_im * bt_re_ref[g]
        z = _dot(jnp.where(lane_b, bb_re, bb_im), times_c(pw_re, pw_im), HIGHEST)
        z = z + jnp.where(zr == zl, d_ref[g], 0.0)

        pr, pi = _int_power(lb_re, lb_im, SSM_CHUNK - 1 - srow, SSM_CHUNK - 1)
        tb_re = jnp.concatenate([bb_re] * SSM_CHUNK, axis=0)
        tb_im = jnp.concatenate([bb_im] * SSM_CHUNK, axis=0)
        ws_g =jnp.where(lane_s, pr * tb_re - pi * tb_im, pr * tb_im + pi * tb_re).astype(BF16)

        for t in range(SSM_CHUNK):
            half = slice((t // gpb) * LANES, (t // gpb + 1) * LANES)
            shift = ((g - t % gpb) * n_ch) % LANES
            move = (lambda a: a) if shift == 0 else (lambda a: pltpu.roll(a, shift, axis=1))
            b_scr[t, g * n_ch:(g + 1) * n_ch, :] = jnp.where(grp_z == g, move(z[:, half]), 0.0)
            wo_ref[0, g * p2:(g + 1) * p2, t * LANES:(t + 1) * LANES] = (
                jnp.where(grp_r == g, move(r1[:, half]), 0.0).astype(BF16))
            ws_ref[0, t * LANES + g * n_ch:t * LANES + (g + 1) * n_ch, g * p2:(g + 1) * p2] = (
                ws_g[t * n_ch:(t + 1) * n_ch, :])

        ar, ai = lb_re, lb_im
        for _ in range(SSM_CHUNK.bit_length() - 1):
            ar, ai = ar * ar - ai * ai, 2.0 * ar * ai
        for k in range(SCAN_ROWS):
            a1_ref[0, k:k + 1, g * p2:(g + 1) * p2] = ar
            a2_ref[0, k:k + 1, g * p2:(g + 1) * p2] = jnp.where(lane_a, -ai, ai)
            ar, ai = ar * ar - ai * ai, 2.0 * ar * ai

    pairs = SSM_CHUNK // 2
    for d in range(pairs):
        r0 = (pairs - 1 - d) * 2 * LANES
        diag = b_scr[2 * d].astype(BF16)
        below = b_scr[2 * d - 1].astype(BF16) if d > 0 else jnp.zeros((LANES, LANES), BF16)
        mst_ref[0, r0:r0 + LANES, :LANES] = diag
        mst_ref[0, r0:r0 + LANES, LANES:] = b_scr[2 * d + 1].astype(BF16)
        mst_ref[0, r0 + LANES:r0 + 2 * LANES, :LANES] = below
        mst_ref[0, r0 + LANES:r0 + 2 * LANES, LANES:] = diag


def _s5_prep_call(a_re, a_im, log_dt, b_re, b_im, c_re, c_im, d_skip):
    p_st, n_ch = a_re.shape[-1], b_re.shape[-1]
    flat = lambda a, k: a.reshape((-1,) + a.shape[a.ndim - k:])
    a_re, a_im, log_dt = flat(a_re, 1), flat(a_im, 1), flat(log_dt, 0)
    b_re, b_im, c_re, c_im = flat(b_re, 2), flat(b_im, 2), flat(c_re, 2), flat(c_im, 2)
    g = a_re.shape[0]
    cn = SSM_CHUNK * n_ch
    p2 = 2 * p_st
    gpb = LANES // n_ch
    nblk = g // gpb
    assert g % gpb == 0
    col = lambda a: jnp.tile(a[:, :, None], (1, 2, 1))
    rowv = lambda a: jnp.tile(a[:, None, :], (1, 1, 2))
    ct = lambda a: jnp.tile(a.transpose(0, 2, 1), (1, 2, 1))
    bt = lambda a: jnp.tile(a.transpose(0, 2, 1), (1, 1, 2))
    d_pad = jnp.pad(d_skip.reshape(g, 1, n_ch), ((0, 0), (0, 0), (0, cn - n_ch)))
    args = (col(a_re), col(a_im), rowv(a_re), rowv(a_im), log_dt.reshape(g, 1, 1),
            ct(c_re), ct(c_im), bt(b_re), bt(b_im), d_pad)
    out_shapes = [jax.ShapeDtypeStruct((nblk, SSM_CHUNK * LANES, 2 * LANES), BF16),
                  jax.ShapeDtypeStruct((nblk, SSM_CHUNK * LANES, gpb * p2), BF16),
                  jax.ShapeDtypeStruct((nblk, gpb * p2, SSM_CHUNK * LANES), BF16),
                  jax.ShapeDtypeStruct((nblk, SCAN_ROWS, gpb * p2), F32),
                  jax.ShapeDtypeStruct((nblk, SCAN_ROWS, gpb * p2), F32)]
    return pl.pallas_call(
        functools.partial(_s5_prep_kernel, n_ch=n_ch, p_st=p_st),
        out_shape=out_shapes,
        grid=(nblk,),
        in_specs=[pl.BlockSpec((gpb,) + a.shape[1:], lambda i: (i, 0, 0)) for a in args],
        out_specs=[pl.BlockSpec((1,) + s.shape[1:], lambda i: (i, 0, 0)) for s in out_shapes],
        scratch_shapes=[pltpu.VMEM((SSM_CHUNK, LANES, LANES), F32)],
        compiler_params=_params(1),
    )(*args)


def _s5_kernel(u_ref, mst_ref, ws_ref, wo_ref, a1_ref, a2_ref, y_ref, xx_ref, sin_ref, *, p_st):
    ncb = xx_ref.shape[0]
    p2 = 2 * p_st
    pair = 2 * LANES
    pairs = SSM_CHUNK // 2
    for t in range(SSM_CHUNK):
        xx_ref[:, t * LANES:(t + 1) * LANES] = u_ref[pl.ds(t, ncb, stride=SSM_CHUNK), :].astype(BF16)
    s_all = _dot(xx_ref[...], ws_ref[0])
    row = lax.broadcasted_iota(jnp.int32, (ncb, p2), 0)
    for g in range(s_all.shape[1] // p2):
        cols = slice(g * p2, (g + 1) * p2)
        x = s_all[:, cols]
        d, k = 1, 0
        while d < ncb:
            xs = jnp.where(row >= d, pltpu.roll(x, d, axis=0), 0.0)
            x = x + a1_ref[0, k:k + 1, cols] * xs + a2_ref[0, k:k + 1, cols] * pltpu.roll(xs, p_st, axis=1)
            d, k = 2 * d, k + 1
        sin_ref[:, cols] = jnp.where(row >= 1, pltpu.roll(x, 1, axis=0), 0.0).astype(BF16)
    for tp in range(pairs):
        yy = (_dot(xx_ref[:, :pair * (tp + 1)], mst_ref[0, (pairs - 1 - tp) * pair:, :])
              + _dot(sin_ref[...], wo_ref[0, :, tp * pair:(tp + 1) * pair]))
        y_ref[pl.ds(2 * tp, ncb, stride=SSM_CHUNK), :] = yy[:, :LANES]
        y_ref[pl.ds(2 * tp + 1, ncb, stride=SSM_CHUNK), :] = yy[:, LANES:]


def _s5_call(u, mats, layer, *, bsz, seq):
    mst, ws, wo, a1, a2 = mats
    t, ssm_w = u.shape
    nblk = ssm_w // LANES
    ncb = seq // SSM_CHUNK
    p_st = LANES // 2
    assert ncb < 2 ** SCAN_ROWS
    wspec = lambda a: pl.BlockSpec((1,) + a.shape[1:], lambda o, b: (layer * nblk + o, 0, 0))
    return pl.pallas_call(
        functools.partial(_s5_kernel, p_st=p_st),
        out_shape=jax.ShapeDtypeStruct((t, ssm_w), F32),
        grid=(nblk, bsz),
        in_specs=[pl.BlockSpec((seq, LANES), lambda o, b: (b, o))] + [wspec(a) for a in mats],
        out_specs=pl.BlockSpec((seq, LANES), lambda o, b: (b, o)),
        scratch_shapes=[pltpu.VMEM((ncb, SSM_CHUNK * LANES), BF16),
                        pltpu.VMEM((ncb, ws.shape[2]), BF16)],
        compiler_params=_params(2),
    )(u, *mats)


def _merge_kernel(x_ref, mod_ref, gpost_ref, ys_ref, at_ref, ga_ref, gb_ref,
                  wglu_ref, wat_ref, wout_ref, o_ref):
    d = x_ref.shape[1]
    _, _, gate = _mod_rows(mod_ref, 1)
    z = _dot(jax.nn.gelu(ys_ref[...]).astype(BF16), wglu_ref[...].astype(BF16))
    y_a = z[:, :d] * jax.nn.sigmoid(z[:, d:])
    y_b = _dot(at_ref[...], wat_ref[...].astype(BF16))
    merged = ga_ref[...].astype(F32) * y_a + gb_ref[...].astype(F32) * y_b
    y = _dot(merged.astype(BF16), wout_ref[...].astype(BF16))
    o_ref[...] = x_ref[...] + gate * _rms(y, gpost_ref[...])


def _merge_call(x, mod_l, g_post, y_ssm, attn, ga, gb, glu_w, attn_w_out, w_out, layer, *, seq, tm):
    t, d = x.shape
    per_b = seq // tm
    row = lambda i: (i, 0)
    ws = [glu_w, attn_w_out, w_out]
    acts = (y_ssm, attn, ga, gb)
    return pl.pallas_call(
        _merge_kernel,
        out_shape=jax.ShapeDtypeStruct((t, d), F32),
        grid=(t // tm,),
        in_specs=[pl.BlockSpec((tm, d), row),
                  pl.BlockSpec((1, 9, d), lambda i: (i // per_b, 0, 0)),
                  _const_spec((1, d))]
                 + [pl.BlockSpec((tm, a.shape[1]), row) for a in acts]
                 + [pl.BlockSpec((None,) + a.shape[1:], lambda i: (layer, 0, 0),
                                 pipeline_mode=pl.Buffered(1)) for a in ws],
        out_specs=pl.BlockSpec((tm, d), row),
        compiler_params=_params(1),
    )(x, mod_l, g_post.reshape(1, d), *acts, *ws)


def kernel(x, c, mod_w, mod_b, norm_pre, norm_post, ffn_w_in, ffn_w_out, mix_w_in, forget_b,
           ssm_a_re, ssm_a_im, ssm_log_dt, ssm_b_re, ssm_b_im, ssm_c_re, ssm_c_im, ssm_d,
           glu_w, attn_w_out, mix_w_out):
    bsz, seq, d = x.shape
    depth = mod_w.shape[0]
    heads = forget_b.shape[1]
    dh = attn_w_out.shape[1] // heads
    groups = ssm_a_re.shape[1]
    ssm_w = ssm_d.shape[1]
    assert heads % 2 == 0 and 2 * dh == LANES and seq % SSM_CHUNK == 0
    tm = min(512, seq)
    tq = min(1024, seq)
    tk = tq // 2
    assert seq % tm == 0 and seq % tq == 0

    mod = _mod_call(c, mod_w, mod_b).reshape(depth, bsz, 9, d)
    mats = _s5_prep_call(ssm_a_re, ssm_a_im, ssm_log_dt, ssm_b_re, ssm_b_im, ssm_c_re, ssm_c_im, ssm_d)
    xt = x.reshape(bsz * seq, d)
    for l in range(depth):
        mod_l = mod[l]
        xt = _ffn_call(xt, mod_l, norm_pre[l, 0], norm_post[l, 0], ffn_w_in, ffn_w_out, l, 0,
                       sub=0, seq=seq, tm=tm)
        u, q_aug, k_aug, vt, ga, gb = _inproj_call(xt, mod_l, norm_pre[l, 1], mix_w_in, l, forget_b[l],
                                                   heads=heads, dh=dh, ssm_w=ssm_w, bsz=bsz, seq=seq,
                                                   tm=tm, tk=tk)
        attn = _attn_call(q_aug, k_aug, vt, heads=heads, dh=dh, bsz=bsz, seq=seq, tq=tq, tk=tk)
        y_ssm = _s5_call(u, mats, l, bsz=bsz, seq=seq)
        xt = _merge_call(xt, mod_l, norm_post[l, 1], y_ssm, attn, ga, gb, glu_w, attn_w_out,
                         mix_w_out, l, seq=seq, tm=tm)
        xt = _ffn_call(xt, mod_l, norm_pre[l, 2], norm_post[l, 2], ffn_w_in, ffn_w_out, l, 1,
                       sub=2, seq=seq, tm=tm)
    return xt.reshape(bsz, seq, d)
```

```python
import functools
import math

import jax
import jax.numpy as jnp
from jax import lax
from jax.experimental import pallas as pl
from jax.experimental.pallas import tpu as pltpu

F32 = jnp.float32
BF16 = jnp.bfloat16

RMS_EPS = 1e-6
FFN_RES = 0.5
A_RE_MAX = -1e-4
LOG2E = math.log2(math.e)
LANES = 128
HEAD_PAD = 128
V_EXTRA = 16
SSM_CHUNK = 16
SCAN_ROWS = 16
VMEM_LIMIT = 56 * 1024 * 1024
HIGHEST = lax.Precision.HIGHEST


def _dot(a, b, precision=None):
    return jnp.dot(a, b, preferred_element_type=F32, precision=precision)


def _rms(x, g):
    ms = jnp.mean(x * x, axis=-1, keepdims=True)
    return x * lax.rsqrt(ms + RMS_EPS) * g


def _silu(x):
    return x * jax.nn.sigmoid(x)


def _params(n_axes, semantics="arbitrary"):
    return pltpu.CompilerParams(dimension_semantics=(semantics,) * n_axes,
                                vmem_limit_bytes=VMEM_LIMIT)


def _const_spec(shape):
    zeros = (0,) * len(shape)
    return pl.BlockSpec(shape, lambda *_: zeros, pipeline_mode=pl.Buffered(1))


def _mod_kernel(c_ref, w_ref, b_ref, o_ref):
    sc = _silu(c_ref[...]).astype(BF16)
    o_ref[0] = _dot(sc, w_ref[0].astype(BF16)) + b_ref[0]


def _mod_call(c, mod_w, mod_b):
    depth, d, n = mod_w.shape
    bsz = c.shape[0]
    rows = 8
    c_pad = jnp.zeros((rows, d), F32).at[:bsz].set(c)
    tn = n // 8 if n % (8 * LANES) == 0 else n
    out = pl.pallas_call(
        _mod_kernel,
        out_shape=jax.ShapeDtypeStruct((depth, rows, n), F32),
        grid=(depth, n // tn),
        in_specs=[pl.BlockSpec((rows, d), lambda l, j: (0, 0)),
                  pl.BlockSpec((1, d, tn), lambda l, j: (l, 0, j)),
                  pl.BlockSpec((1, 1, tn), lambda l, j: (l, 0, j))],
        out_specs=pl.BlockSpec((1, rows, tn), lambda l, j: (l, 0, j)),
        compiler_params=_params(2),
    )(c_pad, mod_w, mod_b.reshape(depth, 1, n))
    return out[:, :bsz]


def _mod_rows(mod_ref, sub):
    shift = mod_ref[0, 3 * sub:3 * sub + 1, :]
    scale = mod_ref[0, 3 * sub + 1:3 * sub + 2, :]
    gate = mod_ref[0, 3 * sub + 2:3 * sub + 3, :]
    return shift, scale, gate


def _ffn_kernel(x_ref, mod_ref, gpre_ref, gpost_ref, wi_ref, wo_ref, o_ref, acc_ref, *, sub, tk):
    x = x_ref[...]
    shift, scale, gate = _mod_rows(mod_ref, sub)
    h = (_rms(x, gpre_ref[...]) * (1.0 + scale) + shift).astype(BF16)
    d_ff = wo_ref.shape[0]
    for k in range(d_ff // tk):
        g = _dot(h, wi_ref[:, k * tk:(k + 1) * tk].astype(BF16))
        u = _dot(h, wi_ref[:, d_ff + k * tk:d_ff + (k + 1) * tk].astype(BF16))
        a = (_silu(g) * u).astype(BF16)
        contrib = _dot(a, wo_ref[k * tk:(k + 1) * tk, :].astype(BF16))
        if k == 0:
            acc_ref[...] = contrib
        else:
            acc_ref[...] += contrib
    o_ref[...] = x + (FFN_RES * gate) * _rms(acc_ref[...], gpost_ref[...])


def _ffn_call(x, mod_l, g_pre, g_post, w_in_all, w_out_all, layer, which, *, sub, seq, tm):
    t, d = x.shape
    d_ff = w_out_all.shape[2]
    tk = 256 if d_ff % 256 == 0 else d_ff
    per_b = seq // tm
    row = lambda i: (i, 0)
    pick = lambda i: (layer, which, 0, 0)
    return pl.pallas_call(
        functools.partial(_ffn_kernel, sub=sub, tk=tk),
        out_shape=jax.ShapeDtypeStruct((t, d), F32),
        grid=(t // tm,),
        in_specs=[pl.BlockSpec((tm, d), row),
                  pl.BlockSpec((1, 9, d), lambda i: (i // per_b, 0, 0)),
                  _const_spec((1, d)), _const_spec((1, d)),
                  pl.BlockSpec((None, None, d, 2 * d_ff), pick, pipeline_mode=pl.Buffered(1)),
                  pl.BlockSpec((None, None, d_ff, d), pick, pipeline_mode=pl.Buffered(1))],
        out_specs=pl.BlockSpec((tm, d), row),
        scratch_shapes=[pltpu.VMEM((tm, d), F32)],
        compiler_params=_params(1),
    )(x, mod_l, g_pre.reshape(1, d), g_post.reshape(1, d), w_in_all, w_out_all)


def _split3(c):
    hi = c.astype(BF16).astype(F32)
    r = c - hi
    mid = r.astype(BF16).astype(F32)
    lo = (r - mid).astype(BF16).astype(F32)
    return hi, mid, lo


def _inproj_kernel(x_ref, mod_ref, gpre_ref, fb_ref, wl_ref, wf_ref, wga_ref, wgb_ref,
                   u_ref, q_ref, k_ref, vt_ref, ga_ref, gb_ref, carry_ref,
                   *, heads, dh, tk, ssm_w):
    tm = x_ref.shape[0]
    aw = heads * dh
    w_cols = lambda start, n: wl_ref[:, start:start + n].astype(BF16)

    @pl.when(pl.program_id(1) == 0)
    def _():
        carry_ref[...] = jnp.zeros_like(carry_ref)

    shift, scale, _ = _mod_rows(mod_ref, 1)
    h = (_rms(x_ref[...], gpre_ref[...]) * (1.0 + scale) + shift).astype(BF16)

    lane = lax.broadcasted_iota(jnp.int32, (tm, LANES), 1)
    log_f = jax.nn.log_sigmoid(_dot(h, wf_ref[...]) + fb_ref[...])
    hi, mid, lo = _split3(jnp.where(lane < heads, log_f, 0.0))
    packed = hi + pltpu.roll(mid, heads, axis=1) + pltpu.roll(lo, 2 * heads, axis=1)
    r = lax.broadcasted_iota(jnp.int32, (tm, tm), 0)
    c = lax.broadcasted_iota(jnp.int32, (tm, tm), 1)
    part = _dot((c <= r).astype(BF16), packed.astype(BF16))
    part = part + pltpu.roll(part, LANES - heads, axis=1) + pltpu.roll(part, LANES - 2 * heads, axis=1)
    cum = jnp.where(lane < heads, part, 0.0) + carry_ref[...]
    carry_ref[...] = cum[tm - 1:tm, :]

    q = _dot(h, w_cols(ssm_w, aw)) * (dh ** -0.5 * LOG2E)
    k = _dot(h, w_cols(ssm_w + aw, aw))
    ones_q = jnp.where((lane >= dh) & (lane < dh + 3), 1.0, 0.0)
    ones_k = jnp.where((lane >= dh + 3) & (lane < dh + 6), 1.0, 0.0)
    for hd in range(heads):
        hi, mid, lo = _split3(cum[:, hd:hd + 1] * LOG2E)
        q_aug = jnp.where(lane == dh + 3, hi, jnp.where(lane == dh + 4, mid,
                          jnp.where(lane == dh + 5, lo, ones_q)))
        k_aug = jnp.where(lane == dh, -hi, jnp.where(lane == dh + 1, -mid,
                          jnp.where(lane == dh + 2, -lo, ones_k)))
        src = slice((hd // 2) * LANES, (hd // 2 + 1) * LANES)
        move = (lambda a: a) if hd % 2 == 0 else (lambda a: pltpu.roll(a, dh, axis=1))
        dst = slice(hd * HEAD_PAD, (hd + 1) * HEAD_PAD)
        q_ref[:, dst] = jnp.where(lane < dh, move(q[:, src]), q_aug).astype(BF16)
        k_ref[:, dst] = jnp.where(lane < dh, move(k[:, src]), k_aug).astype(BF16)

    v_t = _dot(h, w_cols(ssm_w + 2 * aw, aw)).T.astype(BF16)
    ones_rows = (lax.broadcasted_iota(jnp.int32, (V_EXTRA, tk), 0) == 0).astype(BF16)
    for hd in range(heads):
        r0 = (hd % 2) * (dh + V_EXTRA)
        for blk in range(tm // tk):
            vt_ref[hd // 2, blk, r0:r0 + dh, :] = v_t[hd * dh:(hd + 1) * dh, blk * tk:(blk + 1) * tk]
            vt_ref[hd // 2, blk, r0 + dh:r0 + dh + V_EXTRA, :] = ones_rows

    u_ref[...] = _dot(h, w_cols(0, ssm_w))
    ga_ref[...] = jax.nn.sigmoid(_dot(h, wga_ref[...])).astype(BF16)
    gb_ref[...] = jax.nn.sigmoid(_dot(h, wgb_ref[...])).astype(BF16)


def _inproj_call(x, mod_l, g_pre, w_in_all, layer, forget_b, *, heads, dh, ssm_w, bsz, seq, tm, tk):
    t, d = x.shape
    aw = heads * dh
    n_lead = ssm_w + 3 * aw
    assert n_lead % LANES == 0
    w = w_in_all[layer]
    wf = jnp.pad(w[:, n_lead:n_lead + heads], ((0, 0), (0, LANES - heads))).astype(BF16)
    wga = w[:, n_lead + heads:n_lead + heads + d].astype(BF16)
    wgb = w[:, n_lead + heads + d:].astype(BF16)
    ws = [wf, wga, wgb]
    fb = jnp.pad(forget_b.reshape(1, heads), ((0, 0), (0, LANES - heads)))
    per_b = seq // tm
    row = lambda b, i: (b * per_b + i, 0)
    hp = heads * HEAD_PAD
    tok = lambda n, dt: (jax.ShapeDtypeStruct((t, n), dt), pl.BlockSpec((tm, n), row))
    vrows = 2 * (dh + V_EXTRA)
    vt = (jax.ShapeDtypeStruct((bsz, heads // 2, seq // tk, vrows, tk), BF16),
          pl.BlockSpec((None, heads // 2, tm // tk, vrows, tk), lambda b, i: (b, 0, i, 0, 0)))
    outs = [tok(ssm_w, F32), tok(hp, BF16), tok(hp, BF16), vt, tok(d, BF16), tok(d, BF16)]
    return pl.pallas_call(
        functools.partial(_inproj_kernel, heads=heads, dh=dh, tk=tk, ssm_w=ssm_w),
        out_shape=[o[0] for o in outs],
        grid=(bsz, per_b),
        in_specs=[pl.BlockSpec((tm, d), row),
                  pl.BlockSpec((1, 9, d), lambda b, i: (b, 0, 0)),
                  _const_spec((1, d)), _const_spec((1, LANES)),
                  _const_spec((d, n_lead))] + [_const_spec(a.shape) for a in ws],
        out_specs=[o[1] for o in outs],
        scratch_shapes=[pltpu.VMEM((1, LANES), F32)],
        compiler_params=_params(2),
    )(x, mod_l, g_pre.reshape(1, d), fb, w[:, :n_lead], *ws)


def _attn_kernel(q_ref, k_ref, vt_ref, o_ref, s_ref, mx_ref, m_ref, acc_ref, *, tq, tk, dh):
    i = pl.program_id(2)
    assert tq == 2 * tk
    visible = (lax.broadcasted_iota(jnp.int32, (tk, tk), 0)
               <= lax.broadcasted_iota(jnp.int32, (tk, tk), 1))
    head_cols = [slice(hh * HEAD_PAD, (hh + 1) * HEAD_PAD) for hh in range(2)]
    both, late = (0, 1), (1,)

    def scores(j, slot, halves=both, masked_half=None, hds=both):
        rows = pl.ds(pl.multiple_of(j * tk, tk), tk)
        for hh in hds:
            for hf in halves:
                q = q_ref[hf * tk:(hf + 1) * tk, head_cols[hh]]
                st = lax.dot_general(k_ref[rows, head_cols[hh]], q, (((1,), (1,)), ((), ())),
                                     preferred_element_type=F32)
                if hf == masked_half:
                    st = jnp.where(visible, st, -jnp.inf)
                s_ref[slot, hh, hf] = st
                mx_ref[slot, hh, hf] = jnp.max(st, axis=0, keepdims=True)

    vrows = dh + V_EXTRA

    def softmax_pv(j, slot, halves=both, hds=both):
        for hh in hds:
            for hf in halves:
                m = m_ref[hh, hf]
                m_new = jnp.maximum(m, mx_ref[slot, hh, hf])
                m_ref[hh, hf] = m_new
                p = jnp.exp2(s_ref[slot, hh, hf] - m_new)
                acc_ref[hh, hf] = jnp.exp2(m - m_new) * acc_ref[hh, hf] + _dot(
                    vt_ref[j, hh * vrows:(hh + 1) * vrows, :], p.astype(BF16))

    d0, d1 = 2 * i, 2 * i + 1
    m_ref[...] = jnp.full(m_ref.shape, -jnp.inf, F32)
    acc_ref[...] = jnp.zeros(acc_ref.shape, F32)
    scores(d0, 0, masked_half=0)

    def pair(t):
        blk0 = jnp.where(t == 0, d0, 2 * t - 1)
        for nxt, cur, slot in ((2 * t, blk0, 0), (2 * t + 1, 2 * t, 1)):
            for hh in both:
                for hf in both:
                    scores(nxt, 1 - slot, (hf,), hds=(hh,))
                    softmax_pv(cur, slot, (hf,), hds=(hh,))

    def two_pairs(u, carry):
        pair(2 * u)
        pair(2 * u + 1)
        return carry

    lax.fori_loop(0, lax.shift_right_logical(i, 1), two_pairs, 0)

    @pl.when((i & 1) == 1)
    def _():
        pair(i - 1)

    for hh in both:
        scores(d1, 1, late, masked_half=1, hds=(hh,))
        softmax_pv(jnp.where(i == 0, d0, 2 * i - 1), 0, hds=(hh,))
    softmax_pv(d1, 1, late)
    for hf in both:
        o_t = jnp.concatenate([acc_ref[hh, hf, :dh] / acc_ref[hh, hf, dh:dh + 1] for hh in range(2)],
                              axis=0)
        o_ref[hf * tk:(hf + 1) * tk, :] = o_t.T.astype(BF16)


def _attn_call(q_aug, k_aug, vt, *, heads, dh, bsz, seq, tq, tk):
    t = q_aug.shape[0]
    nq = seq // tq
    nkv = seq // tk
    pair = 2 * HEAD_PAD
    return pl.pallas_call(
        functools.partial(_attn_kernel, tq=tq, tk=tk, dh=dh),
        out_shape=jax.ShapeDtypeStruct((t, heads * dh), BF16),
        grid=(bsz, heads // 2, nq),
        in_specs=[pl.BlockSpec((tq, pair), lambda b, h, i: (b * nq + i, h)),
                  pl.BlockSpec((seq, pair), lambda b, h, i: (b, h)),
                  pl.BlockSpec((None, None, nkv, vt.shape[3], tk), lambda b, h, i: (b, h, 0, 0, 0))],
        out_specs=pl.BlockSpec((tq, 2 * dh), lambda b, h, i: (b * nq + i, h)),
        scratch_shapes=[pltpu.VMEM((2, 2, 2, tk, tk), F32),
                        pltpu.VMEM((2, 2, 2, 1, tk), F32),
                        pltpu.VMEM((2, 2, 1, tk), F32),
                        pltpu.VMEM((2, 2, dh + V_EXTRA, tk), F32)],
        compiler_params=_params(3),
    )(q_aug, k_aug, vt)


def _int_power(re, im, e, e_max):
    pr, pi = jnp.ones(e.shape, F32), jnp.zeros(e.shape, F32)
    bit = 0
    while (1 << bit) <= e_max:
        on = (lax.shift_right_logical(e, bit) & 1) == 1
        pr, pi = jnp.where(on, pr * re - pi * im, pr), jnp.where(on, pr * im + pi * re, pi)
        re, im = re * re - im * im, 2.0 * re * im
        bit += 1
    return pr, pi


def _s5_prep_kernel(re_c_ref, im_c_ref, re_r_ref, im_r_ref, ldt_ref, ct_re_ref, ct_im_ref,
                    bt_re_ref, bt_im_ref, d_ref,
                    mst_ref, ws_ref, wo_ref, a1_ref, a2_ref, b_scr, *, n_ch, p_st):
    cn = SSM_CHUNK * n_ch
    p2 = 2 * p_st
    gpb = LANES // n_ch
    assert cn == 2 * LANES and p2 == LANES
    shift_bits = n_ch.bit_length() - 1
    tau = lax.shift_right_logical(lax.broadcasted_iota(jnp.int32, (p2, cn), 1), shift_bits)
    top = lax.broadcasted_iota(jnp.int32, (p2, cn), 0) < p_st
    lane_b = lax.broadcasted_iota(jnp.int32, (n_ch, p2), 1) < p_st
    zr = lax.broadcasted_iota(jnp.int32, (n_ch, cn), 0)
    zl = lax.broadcasted_iota(jnp.int32, (n_ch, cn), 1)
    srow = lax.shift_right_logical(lax.broadcasted_iota(jnp.int32, (cn, p2), 0), shift_bits)
    lane_s = lax.broadcasted_iota(jnp.int32, (cn, p2), 1) < p_st
    lane_a = lax.broadcasted_iota(jnp.int32, (1, p2), 1) < p_st
    grp_z = lax.shift_right_logical(lax.broadcasted_iota(jnp.int32, (n_ch, LANES), 1), shift_bits)
    grp_r = lax.shift_right_logical(lax.broadcasted_iota(jnp.int32, (p2, LANES), 1), shift_bits)
    spread = (lax.broadcasted_iota(jnp.int32, (n_ch, cn), 0)
              == (lax.broadcasted_iota(jnp.int32, (n_ch, cn), 1) & (n_ch - 1))).astype(F32)

    ws_ref[0] = jnp.zeros(ws_ref.shape[1:], BF16)
    for g in range(gpb):
        dt = jnp.exp(ldt_ref[g])
        are_c = jnp.minimum(re_c_ref[g], A_RE_MAX) * dt
        aim_c = im_c_ref[g] * dt
        re_r = jnp.minimum(re_r_ref[g], A_RE_MAX)
        im_r = im_r_ref[g]
        are_r = re_r * dt
        aim_r = im_r * dt

        mag = jnp.exp(are_r)
        lb_re = mag * jnp.cos(aim_r)
        lb_im = mag * jnp.sin(aim_r)
        den = re_r * re_r + im_r * im_r
        xr = lb_re - 1.0
        coef_re = (xr * re_r + lb_im * im_r) / den
        coef_im = (lb_im * re_r - xr * im_r) / den

        ct_re = _dot(ct_re_ref[g], spread, HIGHEST)
        ct_im = _dot(ct_im_ref[g], spread, HIGHEST)

        mag_c = jnp.exp(are_c)
        lam_c = (mag_c * jnp.cos(aim_c), mag_c * jnp.sin(aim_c))
        pw_re, pw_im = _int_power(lam_c[0], lam_c[1], tau, SSM_CHUNK - 1)

        def times_c(pr, pi):
            rr = pr * ct_re - pi * ct_im
            ri = pr * ct_im + pi * ct_re
            return jnp.where(top, rr, -ri)

        r1 = times_c(pw_re * lam_c[0] - pw_im * lam_c[1], pw_re * lam_c[1] + pw_im * lam_c[0])
        bb_re = coef_re * bt_re_ref[g] - coef_im * bt_im_ref[g]
        bb_im = coef_re * bt_im_ref[g] + coef_im * bt_re_ref[g]
        z = _dot(jnp.where(lane_b, bb_re, bb_im), times_c(pw_re, pw_im), HIGHEST)
        z = z + jnp.where(zr == zl, d_ref[g], 0.0)

        pr, pi = _int_power(lb_re, lb_im, SSM_CHUNK - 1 - srow, SSM_CHUNK - 1)
        tb_re = jnp.concatenate([bb_re] * SSM_CHUNK, axis=0)
        tb_im = jnp.concatenate([bb_im] * SSM_CHUNK, axis=0)
        ws_g =jnp.where(lane_s, pr * tb_re - pi * tb_im, pr * tb_im + pi * tb_re).astype(BF16)

        for t in range(SSM_CHUNK):
            half = slice((t // gpb) * LANES, (t // gpb + 1) * LANES)
            shift = ((g - t % gpb) * n_ch) % LANES
            move = (lambda a: a) if shift == 0 else (lambda a: pltpu.roll(a, shift, axis=1))
            b_scr[t, g * n_ch:(g + 1) * n_ch, :] = jnp.where(grp_z == g, move(z[:, half]), 0.0)
            wo_ref[0, g * p2:(g + 1) * p2, t * LANES:(t + 1) * LANES] = (
                jnp.where(grp_r == g, move(r1[:, half]), 0.0).astype(BF16))
            ws_ref[0, t * LANES + g * n_ch:t * LANES + (g + 1) * n_ch, g * p2:(g + 1) * p2] = (
                ws_g[t * n_ch:(t + 1) * n_ch, :])

        ar, ai = lb_re, lb_im
        for _ in range(SSM_CHUNK.bit_length() - 1):
            ar, ai = ar * ar - ai * ai, 2.0 * ar * ai
        for k in range(SCAN_ROWS):
            a1_ref[0, k:k + 1, g * p2:(g + 1) * p2] = ar
            a2_ref[0, k:k + 1, g * p2:(g + 1) * p2] = jnp.where(lane_a, -ai, ai)
            ar, ai = ar * ar - ai * ai, 2.0 * ar * ai

    pairs = SSM_CHUNK // 2
    for d in range(pairs):
        r0 = (pairs - 1 - d) * 2 * LANES
        diag = b_scr[2 * d].astype(BF16)
        below = b_scr[2 * d - 1].astype(BF16) if d > 0 else jnp.zeros((LANES, LANES), BF16)
        mst_ref[0, r0:r0 + LANES, :LANES] = diag
        mst_ref[0, r0:r0 + LANES, LANES:] = b_scr[2 * d + 1].astype(BF16)
        mst_ref[0, r0 + LANES:r0 + 2 * LANES, :LANES] = below
        mst_ref[0, r0 + LANES:r0 + 2 * LANES, LANES:] = diag


def _s5_prep_call(a_re, a_im, log_dt, b_re, b_im, c_re, c_im, d_skip):
    p_st, n_ch = a_re.shape[-1], b_re.shape[-1]
    flat = lambda a, k: a.reshape((-1,) + a.shape[a.ndim - k:])
    a_re, a_im, log_dt = flat(a_re, 1), flat(a_im, 1), flat(log_dt, 0)
    b_re, b_im, c_re, c_im = flat(b_re, 2), flat(b_im, 2), flat(c_re, 2), flat(c_im, 2)
    g = a_re.shape[0]
    cn = SSM_CHUNK * n_ch
    p2 = 2 * p_st
    gpb = LANES // n_ch
    nblk = g // gpb
    assert g % gpb == 0
    col = lambda a: jnp.tile(a[:, :, None], (1, 2, 1))
    rowv = lambda a: jnp.tile(a[:, None, :], (1, 1, 2))
    ct = lambda a: jnp.tile(a.transpose(0, 2, 1), (1, 2, 1))
    bt = lambda a: jnp.tile(a.transpose(0, 2, 1), (1, 1, 2))
    d_pad = jnp.pad(d_skip.reshape(g, 1, n_ch), ((0, 0), (0, 0), (0, cn - n_ch)))
    args = (col(a_re), col(a_im), rowv(a_re), rowv(a_im), log_dt.reshape(g, 1, 1),
            ct(c_re), ct(c_im), bt(b_re), bt(b_im), d_pad)
    out_shapes = [jax.ShapeDtypeStruct((nblk, SSM_CHUNK * LANES, 2 * LANES), BF16),
                  jax.ShapeDtypeStruct((nblk, SSM_CHUNK * LANES, gpb * p2), BF16),
                  jax.ShapeDtypeStruct((nblk, gpb * p2, SSM_CHUNK * LANES), BF16),
                  jax.ShapeDtypeStruct((nblk, SCAN_ROWS, gpb * p2), F32),
                  jax.ShapeDtypeStruct((nblk, SCAN_ROWS, gpb * p2), F32)]
    return pl.pallas_call(
        functools.partial(_s5_prep_kernel, n_ch=n_ch, p_st=p_st),
        out_shape=out_shapes,
        grid=(nblk,),
        in_specs=[pl.BlockSpec((gpb,) + a.shape[1:], lambda i: (i, 0, 0)) for a in args],
        out_specs=[pl.BlockSpec((1,) + s.shape[1:], lambda i: (i, 0, 0)) for s in out_shapes],
        scratch_shapes=[pltpu.VMEM((SSM_CHUNK, LANES, LANES), F32)],
        compiler_params=_params(1),
    )(*args)


def _s5_kernel(u_ref, mst_ref, ws_ref, wo_ref, a1_ref, a2_ref, y_ref, xx_ref, sin_ref, *, p_st):
    ncb = xx_ref.shape[0]
    p2 = 2 * p_st
    pair = 2 * LANES
    pairs = SSM_CHUNK // 2
    for t in range(SSM_CHUNK):
        xx_ref[:, t * LANES:(t + 1) * LANES] = u_ref[pl.ds(t, ncb, stride=SSM_CHUNK), :].astype(BF16)
    s_all = _dot(xx_ref[...], ws_ref[0])
    row = lax.broadcasted_iota(jnp.int32, (ncb, p2), 0)
    for g in range(s_all.shape[1] // p2):
        cols = slice(g * p2, (g + 1) * p2)
        x = s_all[:, cols]
        d, k = 1, 0
        while d < ncb:
            xs = jnp.where(row >= d, pltpu.roll(x, d, axis=0), 0.0)
            x = x + a1_ref[0, k:k + 1, cols] * xs + a2_ref[0, k:k + 1, cols] * pltpu.roll(xs, p_st, axis=1)
            d, k = 2 * d, k + 1
        sin_ref[:, cols] = jnp.where(row >= 1, pltpu.roll(x, 1, axis=0), 0.0).astype(BF16)
    for tp in range(pairs):
        yy = (_dot(xx_ref[:, :pair * (tp + 1)], mst_ref[0, (pairs - 1 - tp) * pair:, :])
              + _dot(sin_ref[...], wo_ref[0, :, tp * pair:(tp + 1) * pair]))
        y_ref[pl.ds(2 * tp, ncb, stride=SSM_CHUNK), :] = yy[:, :LANES]
        y_ref[pl.ds(2 * tp + 1, ncb, stride=SSM_CHUNK), :] = yy[:, LANES:]


def _s5_call(u, mats, layer, *, bsz, seq):
    mst, ws, wo, a1, a2 = mats
    t, ssm_w = u.shape
    nblk = ssm_w // LANES
    ncb = seq // SSM_CHUNK
    p_st = LANES // 2
    assert ncb < 2 ** SCAN_ROWS
    wspec = lambda a: pl.BlockSpec((1,) + a.shape[1:], lambda o, b: (layer * nblk + o, 0, 0))
    return pl.pallas_call(
        functools.partial(_s5_kernel, p_st=p_st),
        out_shape=jax.ShapeDtypeStruct((t, ssm_w), F32),
        grid=(nblk, bsz),
        in_specs=[pl.BlockSpec((seq, LANES), lambda o, b: (b, o))] + [wspec(a) for a in mats],
        out_specs=pl.BlockSpec((seq, LANES), lambda o, b: (b, o)),
        scratch_shapes=[pltpu.VMEM((ncb, SSM_CHUNK * LANES), BF16),
                        pltpu.VMEM((ncb, ws.shape[2]), BF16)],
        compiler_params=_params(2),
    )(u, *mats)


def _merge_kernel(x_ref, mod_ref, gpost_ref, ys_ref, at_ref, ga_ref, gb_ref,
                  wglu_ref, wat_ref, wout_ref, o_ref):
    d = x_ref.shape[1]
    _, _, gate = _mod_rows(mod_ref, 1)
    z = _dot(jax.nn.gelu(ys_ref[...]).astype(BF16), wglu_ref[...].astype(BF16))
    y_a = z[:, :d] * jax.nn.sigmoid(z[:, d:])
    y_b = _dot(at_ref[...], wat_ref[...].astype(BF16))
    merged = ga_ref[...].astype(F32) * y_a + gb_ref[...].astype(F32) * y_b
    y = _dot(merged.astype(BF16), wout_ref[...].astype(BF16))
    o_ref[...] = x_ref[...] + gate * _rms(y, gpost_ref[...])


def _merge_call(x, mod_l, g_post, y_ssm, attn, ga, gb, glu_w, attn_w_out, w_out, layer, *, seq, tm):
    t, d = x.shape
    per_b = seq // tm
    row = lambda i: (i, 0)
    ws = [glu_w, attn_w_out, w_out]
    acts = (y_ssm, attn, ga, gb)
    return pl.pallas_call(
        _merge_kernel,
        out_shape=jax.ShapeDtypeStruct((t, d), F32),
        grid=(t // tm,),
        in_specs=[pl.BlockSpec((tm, d), row),
                  pl.BlockSpec((1, 9, d), lambda i: (i // per_b, 0, 0)),
                  _const_spec((1, d))]
                 + [pl.BlockSpec((tm, a.shape[1]), row) for a in acts]
                 + [pl.BlockSpec((None,) + a.shape[1:], lambda i: (layer, 0, 0),
                                 pipeline_mode=pl.Buffered(1)) for a in ws],
        out_specs=pl.BlockSpec((tm, d), row),
        compiler_params=_params(1),
    )(x, mod_l, g_post.reshape(1, d), *acts, *ws)


def kernel(x, c, mod_w, mod_b, norm_pre, norm_post, ffn_w_in, ffn_w_out, mix_w_in, forget_b,
           ssm_a_re, ssm_a_im, ssm_log_dt, ssm_b_re, ssm_b_im, ssm_c_re, ssm_c_im, ssm_d,
           glu_w, attn_w_out, mix_w_out):
    bsz, seq, d = x.shape
    depth = mod_w.shape[0]
    heads = forget_b.shape[1]
    dh = attn_w_out.shape[1] // heads
    groups = ssm_a_re.shape[1]
    ssm_w = ssm_d.shape[1]
    assert heads % 2 == 0 and 2 * dh == LANES and seq % SSM_CHUNK == 0
    tm = min(512, seq)
    tq = min(1024, seq)
    tk = tq // 2
    assert seq % tm == 0 and seq % tq == 0

    mod = _mod_call(c, mod_w, mod_b).reshape(depth, bsz, 9, d)
    mats = _s5_prep_call(ssm_a_re, ssm_a_im, ssm_log_dt, ssm_b_re, ssm_b_im, ssm_c_re, ssm_c_im, ssm_d)
    xt = x.reshape(bsz * seq, d)
    for l in range(depth):
        mod_l = mod[l]
        xt = _ffn_call(xt, mod_l, norm_pre[l, 0], norm_post[l, 0], ffn_w_in, ffn_w_out, l, 0,
                       sub=0, seq=seq, tm=tm)
        u, q_aug, k_aug, vt, ga, gb = _inproj_call(xt, mod_l, norm_pre[l, 1], mix_w_in, l, forget_b[l],
                                                   heads=heads, dh=dh, ssm_w=ssm_w, bsz=bsz, seq=seq,
                                                   tm=tm, tk=tk)
        attn = _attn_call(q_aug, k_aug, vt, heads=heads, dh=dh, bsz=bsz, seq=seq, tq=tq, tk=tk)
        y_ssm = _s5_call(u, mats, l, bsz=bsz, seq=seq)
        xt = _merge_call(xt, mod_l, norm_post[l, 1], y_ssm, attn, ga, gb, glu_w, attn_w_out,
                         mix_w_out, l, seq=seq, tm=tm)
        xt = _ffn_call(xt, mod_l, norm_pre[l, 2], norm_post[l, 2], ffn_w_in, ffn_w_out, l, 1,
                       sub=2, seq=seq, tm=tm)
    return xt.reshape(bsz, seq, d)
```

```python
import functools
import math

import jax
import jax.numpy as jnp
from jax import lax
from jax.experimental import pallas as pl
from jax.experimental.pallas import tpu as pltpu

F32 = jnp.float32
BF16 = jnp.bfloat16

RMS_EPS = 1e-6
FFN_RES = 0.5
A_RE_MAX = -1e-4
LOG2E = math.log2(math.e)
LANES = 128
HEAD_PAD = 128
V_EXTRA = 16
SSM_CHUNK = 16
SCAN_ROWS = 16
VMEM_LIMIT = 56 * 1024 * 1024
HIGHEST = lax.Precision.HIGHEST


def _dot(a, b, precision=None):
    return jnp.dot(a, b, preferred_element_type=F32, precision=precision)


def _rms(x, g):
    ms = jnp.mean(x * x, axis=-1, keepdims=True)
    return x * lax.rsqrt(ms + RMS_EPS) * g


def _silu(x):
    return x * jax.nn.sigmoid(x)


def _params(n_axes, semantics="arbitrary"):
    return pltpu.CompilerParams(dimension_semantics=(semantics,) * n_axes,
                                vmem_limit_bytes=VMEM_LIMIT)


def _const_spec(shape):
    zeros = (0,) * len(shape)
    return pl.BlockSpec(shape, lambda *_: zeros, pipeline_mode=pl.Buffered(1))


def _mod_kernel(c_ref, w_ref, b_ref, o_ref):
    sc = _silu(c_ref[...]).astype(BF16)
    o_ref[0] = _dot(sc, w_ref[0].astype(BF16)) + b_ref[0]


def _mod_call(c, mod_w, mod_b):
    depth, d, n = mod_w.shape
    bsz = c.shape[0]
    rows = 8
    c_pad = jnp.zeros((rows, d), F32).at[:bsz].set(c)
    tn = n // 8 if n % (8 * LANES) == 0 else n
    out = pl.pallas_call(
        _mod_kernel,
        out_shape=jax.ShapeDtypeStruct((depth, rows, n), F32),
        grid=(depth, n // tn),
        in_specs=[pl.BlockSpec((rows, d), lambda l, j: (0, 0)),
                  pl.BlockSpec((1, d, tn), lambda l, j: (l, 0, j)),
                  pl.BlockSpec((1, 1, tn), lambda l, j: (l, 0, j))],
        out_specs=pl.BlockSpec((1, rows, tn), lambda l, j: (l, 0, j)),
        compiler_params=_params(2),
    )(c_pad, mod_w, mod_b.reshape(depth, 1, n))
    return out[:, :bsz]


def _mod_rows(mod_ref, sub):
    shift = mod_ref[0, 3 * sub:3 * sub + 1, :]
    scale = mod_ref[0, 3 * sub + 1:3 * sub + 2, :]
    gate = mod_ref[0, 3 * sub + 2:3 * sub + 3, :]
    return shift, scale, gate


def _ffn_kernel(x_ref, mod_ref, gpre_ref, gpost_ref, wi_ref, wo_ref, o_ref, acc_ref, *, sub, tk):
    x = x_ref[...]
    shift, scale, gate = _mod_rows(mod_ref, sub)
    h = (_rms(x, gpre_ref[...]) * (1.0 + scale) + shift).astype(BF16)
    d_ff = wo_ref.shape[0]
    for k in range(d_ff // tk):
        g = _dot(h, wi_ref[:, k * tk:(k + 1) * tk].astype(BF16))
        u = _dot(h, wi_ref[:, d_ff + k * tk:d_ff + (k + 1) * tk].astype(BF16))
        a = (_silu(g) * u).astype(BF16)
        contrib = _dot(a, wo_ref[k * tk:(k + 1) * tk, :].astype(BF16))
        if k == 0:
            acc_ref[...] = contrib
        else:
            acc_ref[...] += contrib
    o_ref[...] = x + (FFN_RES * gate) * _rms(acc_ref[...], gpost_ref[...])


def _ffn_call(x, mod_l, g_pre, g_post, w_in_all, w_out_all, layer, which, *, sub, seq, tm):
    t, d = x.shape
    d_ff = w_out_all.shape[2]
    tk = 256 if d_ff % 256 == 0 else d_ff
    per_b = seq // tm
    row = lambda i: (i, 0)
    pick = lambda i: (layer, which, 0, 0)
    return pl.pallas_call(
        functools.partial(_ffn_kernel, sub=sub, tk=tk),
        out_shape=jax.ShapeDtypeStruct((t, d), F32),
        grid=(t // tm,),
        in_specs=[pl.BlockSpec((tm, d), row),
                  pl.BlockSpec((1, 9, d), lambda i: (i // per_b, 0, 0)),
                  _const_spec((1, d)), _const_spec((1, d)),
                  pl.BlockSpec((None, None, d, 2 * d_ff), pick, pipeline_mode=pl.Buffered(1)),
                  pl.BlockSpec((None, None, d_ff, d), pick, pipeline_mode=pl.Buffered(1))],
        out_specs=pl.BlockSpec((tm, d), row),
        scratch_shapes=[pltpu.VMEM((tm, d), F32)],
        compiler_params=_params(1),
    )(x, mod_l, g_pre.reshape(1, d), g_post.reshape(1, d), w_in_all, w_out_all)


def _split3(c):
    hi = c.astype(BF16).astype(F32)
    r = c - hi
    mid = r.astype(BF16).astype(F32)
    lo = (r - mid).astype(BF16).astype(F32)
    return hi, mid, lo


def _inproj_kernel(x_ref, mod_ref, gpre_ref, fb_ref, wl_ref, wf_ref, wga_ref, wgb_ref,
                   u_ref, q_ref, k_ref, vt_ref, ga_ref, gb_ref, carry_ref,
                   *, heads, dh, tk, ssm_w):
    tm = x_ref.shape[0]
    aw = heads * dh
    w_cols = lambda start, n: wl_ref[:, start:start + n].astype(BF16)

    @pl.when(pl.program_id(1) == 0)
    def _():
        carry_ref[...] = jnp.zeros_like(carry_ref)

    shift, scale, _ = _mod_rows(mod_ref, 1)
    h = (_rms(x_ref[...], gpre_ref[...]) * (1.0 + scale) + shift).astype(BF16)

    lane = lax.broadcasted_iota(jnp.int32, (tm, LANES), 1)
    log_f = jax.nn.log_sigmoid(_dot(h, wf_ref[...]) + fb_ref[...])
    hi, mid, lo = _split3(jnp.where(lane < heads, log_f, 0.0))
    packed = hi + pltpu.roll(mid, heads, axis=1) + pltpu.roll(lo, 2 * heads, axis=1)
    r = lax.broadcasted_iota(jnp.int32, (tm, tm), 0)
    c = lax.broadcasted_iota(jnp.int32, (tm, tm), 1)
    part = _dot((c <= r).astype(BF16), packed.astype(BF16))
    part = part + pltpu.roll(part, LANES - heads, axis=1) + pltpu.roll(part, LANES - 2 * heads, axis=1)
    cum = jnp.where(lane < heads, part, 0.0) + carry_ref[...]
    carry_ref[...] = cum[tm - 1:tm, :]

    q = _dot(h, w_cols(ssm_w, aw)) * (dh ** -0.5 * LOG2E)
    k = _dot(h, w_cols(ssm_w + aw, aw))
    ones_q = jnp.where((lane >= dh) & (lane < dh + 3), 1.0, 0.0)
    ones_k = jnp.where((lane >= dh + 3) & (lane < dh + 6), 1.0, 0.0)
    for hd in range(heads):
        hi, mid, lo = _split3(cum[:, hd:hd + 1] * LOG2E)
        q_aug = jnp.where(lane == dh + 3, hi, jnp.where(lane == dh + 4, mid,
                          jnp.where(lane == dh + 5, lo, ones_q)))
        k_aug = jnp.where(lane == dh, -hi, jnp.where(lane == dh + 1, -mid,
                          jnp.where(lane == dh + 2, -lo, ones_k)))
        src = slice((hd // 2) * LANES, (hd // 2 + 1) * LANES)
        move = (lambda a: a) if hd % 2 == 0 else (lambda a: pltpu.roll(a, dh, axis=1))
        dst = slice(hd * HEAD_PAD, (hd + 1) * HEAD_PAD)
        q_ref[:, dst] = jnp.where(lane < dh, move(q[:, src]), q_aug).astype(BF16)
        k_ref[:, dst] = jnp.where(lane < dh, move(k[:, src]), k_aug).astype(BF16)

    v_t = _dot(h, w_cols(ssm_w + 2 * aw, aw)).T.astype(BF16)
    ones_rows = (lax.broadcasted_iota(jnp.int32, (V_EXTRA, tk), 0) == 0).astype(BF16)
    for hd in range(heads):
        r0 = (hd % 2) * (dh + V_EXTRA)
        for blk in range(tm // tk):
            vt_ref[hd // 2, blk, r0:r0 + dh, :] = v_t[hd * dh:(hd + 1) * dh, blk * tk:(blk + 1) * tk]
            vt_ref[hd // 2, blk, r0 + dh:r0 + dh + V_EXTRA, :] = ones_rows

    u_ref[...] = _dot(h, w_cols(0, ssm_w))
    ga_ref[...] = jax.nn.sigmoid(_dot(h, wga_ref[...])).astype(BF16)
    gb_ref[...] = jax.nn.sigmoid(_dot(h, wgb_ref[...])).astype(BF16)


def _inproj_call(x, mod_l, g_pre, w_in_all, layer, forget_b, *, heads, dh, ssm_w, bsz, seq, tm, tk):
    t, d = x.shape
    aw = heads * dh
    n_lead = ssm_w + 3 * aw
    assert n_lead % LANES == 0
    w = w_in_all[layer]
    wf = jnp.pad(w[:, n_lead:n_lead + heads], ((0, 0), (0, LANES - heads))).astype(BF16)
    wga = w[:, n_lead + heads:n_lead + heads + d].astype(BF16)
    wgb = w[:, n_lead + heads + d:].astype(BF16)
    ws = [wf, wga, wgb]
    fb = jnp.pad(forget_b.reshape(1, heads), ((0, 0), (0, LANES - heads)))
    per_b = seq // tm
    row = lambda b, i: (b * per_b + i, 0)
    hp = heads * HEAD_PAD
    tok = lambda n, dt: (jax.ShapeDtypeStruct((t, n), dt), pl.BlockSpec((tm, n), row))
    vrows = 2 * (dh + V_EXTRA)
    vt = (jax.ShapeDtypeStruct((bsz, heads // 2, seq // tk, vrows, tk), BF16),
          pl.BlockSpec((None, heads // 2, tm // tk, vrows, tk), lambda b, i: (b, 0, i, 0, 0)))
    outs = [tok(ssm_w, F32), tok(hp, BF16), tok(hp, BF16), vt, tok(d, BF16), tok(d, BF16)]
    return pl.pallas_call(
        functools.partial(_inproj_kernel, heads=heads, dh=dh, tk=tk, ssm_w=ssm_w),
        out_shape=[o[0] for o in outs],
        grid=(bsz, per_b),
        in_specs=[pl.BlockSpec((tm, d), row),
                  pl.BlockSpec((1, 9, d), lambda b, i: (b, 0, 0)),
                  _const_spec((1, d)), _const_spec((1, LANES)),
                  _const_spec((d, n_lead))] + [_const_spec(a.shape) for a in ws],
        out_specs=[o[1] for o in outs],
        scratch_shapes=[pltpu.VMEM((1, LANES), F32)],
        compiler_params=_params(2),
    )(x, mod_l, g_pre.reshape(1, d), fb, w[:, :n_lead], *ws)


def _attn_kernel(q_ref, k_ref, vt_ref, o_ref, s_ref, mx_ref, m_ref, acc_ref, *, tq, tk, dh):
    i = pl.program_id(2)
    assert tq == 2 * tk
    visible = (lax.broadcasted_iota(jnp.int32, (tk, tk), 0)
               <= lax.broadcasted_iota(jnp.int32, (tk, tk), 1))
    head_cols = [slice(hh * HEAD_PAD, (hh + 1) * HEAD_PAD) for hh in range(2)]
    both, late = (0, 1), (1,)

    def scores(j, slot, halves=both, masked_half=None, hds=both):
        rows = pl.ds(pl.multiple_of(j * tk, tk), tk)
        for hh in hds:
            for hf in halves:
                q = q_ref[hf * tk:(hf + 1) * tk, head_cols[hh]]
                st = lax.dot_general(k_ref[rows, head_cols[hh]], q, (((1,), (1,)), ((), ())),
                                     preferred_element_type=F32)
                if hf == masked_half:
                    st = jnp.where(visible, st, -jnp.inf)
                s_ref[slot, hh, hf] = st
                mx_ref[slot, hh, hf] = jnp.max(st, axis=0, keepdims=True)

    vrows = dh + V_EXTRA

    def softmax_pv(j, slot, halves=both, hds=both):
        for hh in hds:
            for hf in halves:
                m = m_ref[hh, hf]
                m_new = jnp.maximum(m, mx_ref[slot, hh, hf])
                m_ref[hh, hf] = m_new
                p = jnp.exp2(s_ref[slot, hh, hf] - m_new)
                acc_ref[hh, hf] = jnp.exp2(m - m_new) * acc_ref[hh, hf] + _dot(
                    vt_ref[j, hh * vrows:(hh + 1) * vrows, :], p.astype(BF16))

    d0, d1 = 2 * i, 2 * i + 1
    m_ref[...] = jnp.full(m_ref.shape, -jnp.inf, F32)
    acc_ref[...] = jnp.zeros(acc_ref.shape, F32)
    scores(d0, 0, masked_half=0)

    def pair(t):
        blk0 = jnp.where(t == 0, d0, 2 * t - 1)
        for nxt, cur, slot in ((2 * t, blk0, 0), (2 * t + 1, 2 * t, 1)):
            for hh in both:
                for hf in both:
                    scores(nxt, 1 - slot, (hf,), hds=(hh,))
                    softmax_pv(cur, slot, (hf,), hds=(hh,))

    def one_pair(t, carry):
        pair(t)
        return carry

    lax.fori_loop(0, i, one_pair, 0)

    for hh in both:
        scores(d1, 1, late, masked_half=1, hds=(hh,))
        softmax_pv(jnp.where(i == 0, d0, 2 * i - 1), 0, hds=(hh,))
    softmax_pv(d1, 1, late)
    for hf in both:
        o_t = jnp.concatenate([acc_ref[hh, hf, :dh] / acc_ref[hh, hf, dh:dh + 1] for hh in range(2)],
                              axis=0)
        o_ref[hf * tk:(hf + 1) * tk, :] = o_t.T.astype(BF16)


def _attn_call(q_aug, k_aug, vt, *, heads, dh, bsz, seq, tq, tk):
    t = q_aug.shape[0]
    nq = seq // tq
    nkv = seq // tk
    pair = 2 * HEAD_PAD
    return pl.pallas_call(
        functools.partial(_attn_kernel, tq=tq, tk=tk, dh=dh),
        out_shape=jax.ShapeDtypeStruct((t, heads * dh), BF16),
        grid=(bsz, heads // 2, nq),
        in_specs=[pl.BlockSpec((tq, pair), lambda b, h, i: (b * nq + i, h)),
                  pl.BlockSpec((seq, pair), lambda b, h, i: (b, h)),
                  pl.BlockSpec((None, None, nkv, vt.shape[3], tk), lambda b, h, i: (b, h, 0, 0, 0))],
        out_specs=pl.BlockSpec((tq, 2 * dh), lambda b, h, i: (b * nq + i, h)),
        scratch_shapes=[pltpu.VMEM((2, 2, 2, tk, tk), F32),
                        pltpu.VMEM((2, 2, 2, 1, tk), F32),
                        pltpu.VMEM((2, 2, 1, tk), F32),
                        pltpu.VMEM((2, 2, dh + V_EXTRA, tk), F32)],
        compiler_params=_params(3),
    )(q_aug, k_aug, vt)


def _int_power(re, im, e, e_max):
    pr, pi = jnp.ones(e.shape, F32), jnp.zeros(e.shape, F32)
    bit = 0
    while (1 << bit) <= e_max:
        on = (lax.shift_right_logical(e, bit) & 1) == 1
        pr, pi = jnp.where(on, pr * re - pi * im, pr), jnp.where(on, pr * im + pi * re, pi)
        re, im = re * re - im * im, 2.0 * re * im
        bit += 1
    return pr, pi


def _s5_prep_kernel(re_c_ref, im_c_ref, re_r_ref, im_r_ref, ldt_ref, ct_re_ref, ct_im_ref,
                    bt_re_ref, bt_im_ref, d_ref,
                    mst_ref, ws_ref, wo_ref, a1_ref, a2_ref, b_scr, *, n_ch, p_st):
    cn = SSM_CHUNK * n_ch
    p2 = 2 * p_st
    gpb = LANES // n_ch
    assert cn == 2 * LANES and p2 == LANES
    shift_bits = n_ch.bit_length() - 1
    tau = lax.shift_right_logical(lax.broadcasted_iota(jnp.int32, (p2, cn), 1), shift_bits)
    top = lax.broadcasted_iota(jnp.int32, (p2, cn), 0) < p_st
    lane_b = lax.broadcasted_iota(jnp.int32, (n_ch, p2), 1) < p_st
    zr = lax.broadcasted_iota(jnp.int32, (n_ch, cn), 0)
    zl = lax.broadcasted_iota(jnp.int32, (n_ch, cn), 1)
    srow = lax.shift_right_logical(lax.broadcasted_iota(jnp.int32, (cn, p2), 0), shift_bits)
    lane_s = lax.broadcasted_iota(jnp.int32, (cn, p2), 1) < p_st
    lane_a = lax.broadcasted_iota(jnp.int32, (1, p2), 1) < p_st
    grp_z = lax.shift_right_logical(lax.broadcasted_iota(jnp.int32, (n_ch, LANES), 1), shift_bits)
    grp_r = lax.shift_right_logical(lax.broadcasted_iota(jnp.int32, (p2, LANES), 1), shift_bits)
    spread = (lax.broadcasted_iota(jnp.int32, (n_ch, cn), 0)
              == (lax.broadcasted_iota(jnp.int32, (n_ch, cn), 1) & (n_ch - 1))).astype(F32)

    ws_ref[0] = jnp.zeros(ws_ref.shape[1:], BF16)
    for g in range(gpb):
        dt = jnp.exp(ldt_ref[g])
        are_c = jnp.minimum(re_c_ref[g], A_RE_MAX) * dt
        aim_c = im_c_ref[g] * dt
        re_r = jnp.minimum(re_r_ref[g], A_RE_MAX)
        im_r = im_r_ref[g]
        are_r = re_r * dt
        aim_r = im_r * dt

        mag = jnp.exp(are_r)
        lb_re = mag * jnp.cos(aim_r)
        lb_im = mag * jnp.sin(aim_r)
        den = re_r * re_r + im_r * im_r
        xr = lb_re - 1.0
        coef_re = (xr * re_r + lb_im * im_r) / den
        coef_im = (lb_im * re_r - xr * im_r) / den

        ct_re = _dot(ct_re_ref[g], spread, HIGHEST)
        ct_im = _dot(ct_im_ref[g], spread, HIGHEST)

        mag_c = jnp.exp(are_c)
        lam_c = (mag_c * jnp.cos(aim_c), mag_c * jnp.sin(aim_c))
        pw_re, pw_im = _int_power(lam_c[0], lam_c[1], tau, SSM_CHUNK - 1)

        def times_c(pr, pi):
            rr = pr * ct_re - pi * ct_im
            ri = pr * ct_im + pi * ct_re
            return jnp.where(top, rr, -ri)

        r1 = times_c(pw_re * lam_c[0] - pw_im * lam_c[1], pw_re * lam_c[1] + pw_im * lam_c[0])
        bb_re = coef_re * bt_re_ref[g] - coef_im * bt_im_ref[g]
        bb_im = coef_re * bt_im_ref[g] + coef_im * bt_re_ref[g]
        z = _dot(jnp.where(lane_b, bb_re, bb_im), times_c(pw_re, pw_im), HIGHEST)
        z = z + jnp.where(zr == zl, d_ref[g], 0.0)

        pr, pi = _int_power(lb_re, lb_im, SSM_CHUNK - 1 - srow, SSM_CHUNK - 1)
        tb_re = jnp.concatenate([bb_re] * SSM_CHUNK, axis=0)
        tb_im = jnp.concatenate([bb_im] * SSM_CHUNK, axis=0)
        ws_g =jnp.where(lane_s, pr * tb_re - pi * tb_im, pr * tb_im + pi * tb_re).astype(BF16)

        for t in range(SSM_CHUNK):
            half = slice((t // gpb) * LANES, (t // gpb + 1) * LANES)
            shift = ((g - t % gpb) * n_ch) % LANES
            move = (lambda a: a) if shift == 0 else (lambda a: pltpu.roll(a, shift, axis=1))
            b_scr[t, g * n_ch:(g + 1) * n_ch, :] = jnp.where(grp_z == g, move(z[:, half]), 0.0)
            wo_ref[0, g * p2:(g + 1) * p2, t * LANES:(t + 1) * LANES] = (
                jnp.where(grp_r == g, move(r1[:, half]), 0.0).astype(BF16))
            ws_ref[0, t * LANES + g * n_ch:t * LANES + (g + 1) * n_ch, g * p2:(g + 1) * p2] = (
                ws_g[t * n_ch:(t + 1) * n_ch, :])

        ar, ai = lb_re, lb_im
        for _ in range(SSM_CHUNK.bit_length() - 1):
            ar, ai = ar * ar - ai * ai, 2.0 * ar * ai
        for k in range(SCAN_ROWS):
            a1_ref[0, k:k + 1, g * p2:(g + 1) * p2] = ar
            a2_ref[0, k:k + 1, g * p2:(g + 1) * p2] = jnp.where(lane_a, -ai, ai)
            ar, ai = ar * ar - ai * ai, 2.0 * ar * ai

    pairs = SSM_CHUNK // 2
    for d in range(pairs):
        r0 = (pairs - 1 - d) * 2 * LANES
        diag = b_scr[2 * d].astype(BF16)
        below = b_scr[2 * d - 1].astype(BF16) if d > 0 else jnp.zeros((LANES, LANES), BF16)
        mst_ref[0, r0:r0 + LANES, :LANES] = diag
        mst_ref[0, r0:r0 + LANES, LANES:] = b_scr[2 * d + 1].astype(BF16)
        mst_ref[0, r0 + LANES:r0 + 2 * LANES, :LANES] = below
        mst_ref[0, r0 + LANES:r0 + 2 * LANES, LANES:] = diag


def _s5_prep_call(a_re, a_im, log_dt, b_re, b_im, c_re, c_im, d_skip):
    p_st, n_ch = a_re.shape[-1], b_re.shape[-1]
    flat = lambda a, k: a.reshape((-1,) + a.shape[a.ndim - k:])
    a_re, a_im, log_dt = flat(a_re, 1), flat(a_im, 1), flat(log_dt, 0)
    b_re, b_im, c_re, c_im = flat(b_re, 2), flat(b_im, 2), flat(c_re, 2), flat(c_im, 2)
    g = a_re.shape[0]
    cn = SSM_CHUNK * n_ch
    p2 = 2 * p_st
    gpb = LANES // n_ch
    nblk = g // gpb
    assert g % gpb == 0
    col = lambda a: jnp.tile(a[:, :, None], (1, 2, 1))
    rowv = lambda a: jnp.tile(a[:, None, :], (1, 1, 2))
    ct = lambda a: jnp.tile(a.transpose(0, 2, 1), (1, 2, 1))
    bt = lambda a: jnp.tile(a.transpose(0, 2, 1), (1, 1, 2))
    d_pad = jnp.pad(d_skip.reshape(g, 1, n_ch), ((0, 0), (0, 0), (0, cn - n_ch)))
    args = (col(a_re), col(a_im), rowv(a_re), rowv(a_im), log_dt.reshape(g, 1, 1),
            ct(c_re), ct(c_im), bt(b_re), bt(b_im), d_pad)
    out_shapes = [jax.ShapeDtypeStruct((nblk, SSM_CHUNK * LANES, 2 * LANES), BF16),
                  jax.ShapeDtypeStruct((nblk, SSM_CHUNK * LANES, gpb * p2), BF16),
                  jax.ShapeDtypeStruct((nblk, gpb * p2, SSM_CHUNK * LANES), BF16),
                  jax.ShapeDtypeStruct((nblk, SCAN_ROWS, gpb * p2), F32),
                  jax.ShapeDtypeStruct((nblk, SCAN_ROWS, gpb * p2), F32)]
    return pl.pallas_call(
        functools.partial(_s5_prep_kernel, n_ch=n_ch, p_st=p_st),
        out_shape=out_shapes,
        grid=(nblk,),
        in_specs=[pl.BlockSpec((gpb,) + a.shape[1:], lambda i: (i, 0, 0)) for a in args],
        out_specs=[pl.BlockSpec((1,) + s.shape[1:], lambda i: (i, 0, 0)) for s in out_shapes],
        scratch_shapes=[pltpu.VMEM((SSM_CHUNK, LANES, LANES), F32)],
        compiler_params=_params(1),
    )(*args)


def _s5_kernel(u_ref, mst_ref, ws_ref, wo_ref, a1_ref, a2_ref, y_ref, xx_ref, sin_ref, *, p_st):
    ncb = xx_ref.shape[0]
    p2 = 2 * p_st
    pair = 2 * LANES
    pairs = SSM_CHUNK // 2
    for t in range(SSM_CHUNK):
        xx_ref[:, t * LANES:(t + 1) * LANES] = u_ref[pl.ds(t, ncb, stride=SSM_CHUNK), :].astype(BF16)
    s_all = _dot(xx_ref[...], ws_ref[0])
    row = lax.broadcasted_iota(jnp.int32, (ncb, p2), 0)
    for g in range(s_all.shape[1] // p2):
        cols = slice(g * p2, (g + 1) * p2)
        x = s_all[:, cols]
        d, k = 1, 0
        while d < ncb:
            xs = jnp.where(row >= d, pltpu.roll(x, d, axis=0), 0.0)
            x = x + a1_ref[0, k:k + 1, cols] * xs + a2_ref[0, k:k + 1, cols] * pltpu.roll(xs, p_st, axis=1)
            d, k = 2 * d, k + 1
        sin_ref[:, cols] = jnp.where(row >= 1, pltpu.roll(x, 1, axis=0), 0.0).astype(BF16)
    for tp in range(pairs):
        yy = (_dot(xx_ref[:, :pair * (tp + 1)], mst_ref[0, (pairs - 1 - tp) * pair:, :])
              + _dot(sin_ref[...], wo_ref[0, :, tp * pair:(tp + 1) * pair]))
        y_ref[pl.ds(2 * tp, ncb, stride=SSM_CHUNK), :] = yy[:, :LANES]
        y_ref[pl.ds(2 * tp + 1, ncb, stride=SSM_CHUNK), :] = yy[:, LANES:]


def _s5_call(u, mats, layer, *, bsz, seq):
    mst, ws, wo, a1, a2 = mats
    t, ssm_w = u.shape
    nblk = ssm_w // LANES
    ncb = seq // SSM_CHUNK
    p_st = LANES // 2
    assert ncb < 2 ** SCAN_ROWS
    wspec = lambda a: pl.BlockSpec((1,) + a.shape[1:], lambda o, b: (layer * nblk + o, 0, 0))
    return pl.pallas_call(
        functools.partial(_s5_kernel, p_st=p_st),
        out_shape=jax.ShapeDtypeStruct((t, ssm_w), F32),
        grid=(nblk, bsz),
        in_specs=[pl.BlockSpec((seq, LANES), lambda o, b: (b, o))] + [wspec(a) for a in mats],
        out_specs=pl.BlockSpec((seq, LANES), lambda o, b: (b, o)),
        scratch_shapes=[pltpu.VMEM((ncb, SSM_CHUNK * LANES), BF16),
                        pltpu.VMEM((ncb, ws.shape[2]), BF16)],
        compiler_params=_params(2),
    )(u, *mats)


def _merge_kernel(x_ref, mod_ref, gpost_ref, ys_ref, at_ref, ga_ref, gb_ref,
                  wglu_ref, wat_ref, wout_ref, o_ref):
    d = x_ref.shape[1]
    _, _, gate = _mod_rows(mod_ref, 1)
    z = _dot(jax.nn.gelu(ys_ref[...]).astype(BF16), wglu_ref[...].astype(BF16))
    y_a = z[:, :d] * jax.nn.sigmoid(z[:, d:])
    y_b = _dot(at_ref[...], wat_ref[...].astype(BF16))
    merged = ga_ref[...].astype(F32) * y_a + gb_ref[...].astype(F32) * y_b
    y = _dot(merged.astype(BF16), wout_ref[...].astype(BF16))
    o_ref[...] = x_ref[...] + gate * _rms(y, gpost_ref[...])


def _merge_call(x, mod_l, g_post, y_ssm, attn, ga, gb, glu_w, attn_w_out, w_out, layer, *, seq, tm):
    t, d = x.shape
    per_b = seq // tm
    row = lambda i: (i, 0)
    ws = [glu_w, attn_w_out, w_out]
    acts = (y_ssm, attn, ga, gb)
    return pl.pallas_call(
        _merge_kernel,
        out_shape=jax.ShapeDtypeStruct((t, d), F32),
        grid=(t // tm,),
        in_specs=[pl.BlockSpec((tm, d), row),
                  pl.BlockSpec((1, 9, d), lambda i: (i // per_b, 0, 0)),
                  _const_spec((1, d))]
                 + [pl.BlockSpec((tm, a.shape[1]), row) for a in acts]
                 + [pl.BlockSpec((None,) + a.shape[1:], lambda i: (layer, 0, 0),
                                 pipeline_mode=pl.Buffered(1)) for a in ws],
        out_specs=pl.BlockSpec((tm, d), row),
        compiler_params=_params(1),
    )(x, mod_l, g_post.reshape(1, d), *acts, *ws)


def kernel(x, c, mod_w, mod_b, norm_pre, norm_post, ffn_w_in, ffn_w_out, mix_w_in, forget_b,
           ssm_a_re, ssm_a_im, ssm_log_dt, ssm_b_re, ssm_b_im, ssm_c_re, ssm_c_im, ssm_d,
           glu_w, attn_w_out, mix_w_out):
    bsz, seq, d = x.shape
    depth = mod_w.shape[0]
    heads = forget_b.shape[1]
    dh = attn_w_out.shape[1] // heads
    groups = ssm_a_re.shape[1]
    ssm_w = ssm_d.shape[1]
    assert heads % 2 == 0 and 2 * dh == LANES and seq % SSM_CHUNK == 0
    tm = min(512, seq)
    tq = min(1024, seq)
    tk = tq // 2
    assert seq % tm == 0 and seq % tq == 0

    mod = _mod_call(c, mod_w, mod_b).reshape(depth, bsz, 9, d)
    mats = _s5_prep_call(ssm_a_re, ssm_a_im, ssm_log_dt, ssm_b_re, ssm_b_im, ssm_c_re, ssm_c_im, ssm_d)
    xt = x.reshape(bsz * seq, d)
    for l in range(depth):
        mod_l = mod[l]
        xt = _ffn_call(xt, mod_l, norm_pre[l, 0], norm_post[l, 0], ffn_w_in, ffn_w_out, l, 0,
                       sub=0, seq=seq, tm=tm)
        u, q_aug, k_aug, vt, ga, gb = _inproj_call(xt, mod_l, norm_pre[l, 1], mix_w_in, l, forget_b[l],
                                                   heads=heads, dh=dh, ssm_w=ssm_w, bsz=bsz, seq=seq,
                                                   tm=tm, tk=tk)
        attn = _attn_call(q_aug, k_aug, vt, heads=heads, dh=dh, bsz=bsz, seq=seq, tq=tq, tk=tk)
        y_ssm = _s5_call(u, mats, l, bsz=bsz, seq=seq)
        xt = _merge_call(xt, mod_l, norm_post[l, 1], y_ssm, attn, ga, gb, glu_w, attn_w_out,
                         mix_w_out, l, seq=seq, tm=tm)
        xt = _ffn_call(xt, mod_l, norm_pre[l, 2], norm_post[l, 2], ffn_w_in, ffn_w_out, l, 1,
                       sub=2, seq=seq, tm=tm)
    return xt.reshape(bsz, seq, d)
```

```python
import functools
import math

import jax
import jax.numpy as jnp
from jax import lax
from jax.experimental import pallas as pl
from jax.experimental.pallas import tpu as pltpu

F32 = jnp.float32
BF16 = jnp.bfloat16

RMS_EPS = 1e-6
FFN_RES = 0.5
A_RE_MAX = -1e-4
LOG2E = math.log2(math.e)
LANES = 128
HEAD_PAD = 128
V_EXTRA = 16
SSM_CHUNK = 16
SCAN_ROWS = 16
VMEM_LIMIT = 56 * 1024 * 1024
HIGHEST = lax.Precision.HIGHEST


def _dot(a, b, precision=None):
    return jnp.dot(a, b, preferred_element_type=F32, precision=precision)


def _rms(x, g):
    ms = jnp.mean(x * x, axis=-1, keepdims=True)
    return x * lax.rsqrt(ms + RMS_EPS) * g


def _silu(x):
    return x * jax.nn.sigmoid(x)


def _params(n_axes, semantics="arbitrary"):
    return pltpu.CompilerParams(dimension_semantics=(semantics,) * n_axes,
                                vmem_limit_bytes=VMEM_LIMIT)


def _const_spec(shape):
    zeros = (0,) * len(shape)
    return pl.BlockSpec(shape, lambda *_: zeros, pipeline_mode=pl.Buffered(1))


def _mod_kernel(c_ref, w_ref, b_ref, o_ref):
    sc = _silu(c_ref[...]).astype(BF16)
    o_ref[0] = _dot(sc, w_ref[0].astype(BF16)) + b_ref[0]


def _mod_call(c, mod_w, mod_b):
    depth, d, n = mod_w.shape
    bsz = c.shape[0]
    rows = 8
    c_pad = jnp.zeros((rows, d), F32).at[:bsz].set(c)
    tn = n // 8 if n % (8 * LANES) == 0 else n
    out = pl.pallas_call(
        _mod_kernel,
        out_shape=jax.ShapeDtypeStruct((depth, rows, n), F32),
        grid=(depth, n // tn),
        in_specs=[pl.BlockSpec((rows, d), lambda l, j: (0, 0)),
                  pl.BlockSpec((1, d, tn), lambda l, j: (l, 0, j)),
                  pl.BlockSpec((1, 1, tn), lambda l, j: (l, 0, j))],
        out_specs=pl.BlockSpec((1, rows, tn), lambda l, j: (l, 0, j)),
        compiler_params=_params(2),
    )(c_pad, mod_w, mod_b.reshape(depth, 1, n))
    return out[:, :bsz]


def _mod_rows(mod_ref, sub):
    shift = mod_ref[0, 3 * sub:3 * sub + 1, :]
    scale = mod_ref[0, 3 * sub + 1:3 * sub + 2, :]
    gate = mod_ref[0, 3 * sub + 2:3 * sub + 3, :]
    return shift, scale, gate


def _ffn_kernel(x_ref, mod_ref, gpre_ref, gpost_ref, wi_ref, wo_ref, o_ref, acc_ref, *, sub, tk):
    x = x_ref[...]
    shift, scale, gate = _mod_rows(mod_ref, sub)
    h = (_rms(x, gpre_ref[...]) * (1.0 + scale) + shift).astype(BF16)
    d_ff = wo_ref.shape[0]
    for k in range(d_ff // tk):
        g = _dot(h, wi_ref[:, k * tk:(k + 1) * tk].astype(BF16))
        u = _dot(h, wi_ref[:, d_ff + k * tk:d_ff + (k + 1) * tk].astype(BF16))
        a = (_silu(g) * u).astype(BF16)
        contrib = _dot(a, wo_ref[k * tk:(k + 1) * tk, :].astype(BF16))
        if k == 0:
            acc_ref[...] = contrib
        else:
            acc_ref[...] += contrib
    o_ref[...] = x + (FFN_RES * gate) * _rms(acc_ref[...], gpost_ref[...])


def _ffn_call(x, mod_l, g_pre, g_post, w_in_all, w_out_all, layer, which, *, sub, seq, tm):
    t, d = x.shape
    d_ff = w_out_all.shape[2]
    tk = 256 if d_ff % 256 == 0 else d_ff
    per_b = seq // tm
    row = lambda i: (i, 0)
    pick = lambda i: (layer, which, 0, 0)
    return pl.pallas_call(
        functools.partial(_ffn_kernel, sub=sub, tk=tk),
        out_shape=jax.ShapeDtypeStruct((t, d), F32),
        grid=(t // tm,),
        in_specs=[pl.BlockSpec((tm, d), row),
                  pl.BlockSpec((1, 9, d), lambda i: (i // per_b, 0, 0)),
                  _const_spec((1, d)), _const_spec((1, d)),
                  pl.BlockSpec((None, None, d, 2 * d_ff), pick, pipeline_mode=pl.Buffered(1)),
                  pl.BlockSpec((None, None, d_ff, d), pick, pipeline_mode=pl.Buffered(1))],
        out_specs=pl.BlockSpec((tm, d), row),
        scratch_shapes=[pltpu.VMEM((tm, d), F32)],
        compiler_params=_params(1),
    )(x, mod_l, g_pre.reshape(1, d), g_post.reshape(1, d), w_in_all, w_out_all)


def _split3(c):
    hi = c.astype(BF16).astype(F32)
    r = c - hi
    mid = r.astype(BF16).astype(F32)
    lo = (r - mid).astype(BF16).astype(F32)
    return hi, mid, lo


def _inproj_kernel(x_ref, mod_ref, gpre_ref, fb_ref, wl_ref, wf_ref, wga_ref, wgb_ref,
                   u_ref, q_ref, k_ref, vt_ref, ga_ref, gb_ref, carry_ref,
                   *, heads, dh, tk, ssm_w):
    tm = x_ref.shape[0]
    aw = heads * dh
    w_cols = lambda start, n: wl_ref[:, start:start + n].astype(BF16)

    @pl.when(pl.program_id(1) == 0)
    def _():
        carry_ref[...] = jnp.zeros_like(carry_ref)

    shift, scale, _ = _mod_rows(mod_ref, 1)
    h = (_rms(x_ref[...], gpre_ref[...]) * (1.0 + scale) + shift).astype(BF16)

    lane = lax.broadcasted_iota(jnp.int32, (tm, LANES), 1)
    log_f = jax.nn.log_sigmoid(_dot(h, wf_ref[...]) + fb_ref[...])
    hi, mid, lo = _split3(jnp.where(lane < heads, log_f, 0.0))
    packed = hi + pltpu.roll(mid, heads, axis=1) + pltpu.roll(lo, 2 * heads, axis=1)
    r = lax.broadcasted_iota(jnp.int32, (tm, tm), 0)
    c = lax.broadcasted_iota(jnp.int32, (tm, tm), 1)
    part = _dot((c <= r).astype(BF16), packed.astype(BF16))
    part = part + pltpu.roll(part, LANES - heads, axis=1) + pltpu.roll(part, LANES - 2 * heads, axis=1)
    cum = jnp.where(lane < heads, part, 0.0) + carry_ref[...]
    carry_ref[...] = cum[tm - 1:tm, :]

    q = _dot(h, w_cols(ssm_w, aw)) * (dh ** -0.5 * LOG2E)
    k = _dot(h, w_cols(ssm_w + aw, aw))
    ones_q = jnp.where((lane >= dh) & (lane < dh + 3), 1.0, 0.0)
    ones_k = jnp.where((lane >= dh + 3) & (lane < dh + 6), 1.0, 0.0)
    for hd in range(heads):
        hi, mid, lo = _split3(cum[:, hd:hd + 1] * LOG2E)
        q_aug = jnp.where(lane == dh + 3, hi, jnp.where(lane == dh + 4, mid,
                          jnp.where(lane == dh + 5, lo, ones_q)))
        k_aug = jnp.where(lane == dh, -hi, jnp.where(lane == dh + 1, -mid,
                          jnp.where(lane == dh + 2, -lo, ones_k)))
        src = slice((hd // 2) * LANES, (hd // 2 + 1) * LANES)
        move = (lambda a: a) if hd % 2 == 0 else (lambda a: pltpu.roll(a, dh, axis=1))
        dst = slice(hd * HEAD_PAD, (hd + 1) * HEAD_PAD)
        q_ref[:, dst] = jnp.where(lane < dh, move(q[:, src]), q_aug).astype(BF16)
        k_ref[:, dst] = jnp.where(lane < dh, move(k[:, src]), k_aug).astype(BF16)

    v_t = _dot(h, w_cols(ssm_w + 2 * aw, aw)).T.astype(BF16)
    ones_rows = (lax.broadcasted_iota(jnp.int32, (V_EXTRA, tk), 0) == 0).astype(BF16)
    for hd in range(heads):
        r0 = (hd % 2) * (dh + V_EXTRA)
        for blk in range(tm // tk):
            vt_ref[hd // 2, blk, r0:r0 + dh, :] = v_t[hd * dh:(hd + 1) * dh, blk * tk:(blk + 1) * tk]
            vt_ref[hd // 2, blk, r0 + dh:r0 + dh + V_EXTRA, :] = ones_rows

    u_ref[...] = _dot(h, w_cols(0, ssm_w))
    ga_ref[...] = jax.nn.sigmoid(_dot(h, wga_ref[...])).astype(BF16)
    gb_ref[...] = jax.nn.sigmoid(_dot(h, wgb_ref[...])).astype(BF16)


def _inproj_call(x, mod_l, g_pre, w_in_all, layer, forget_b, *, heads, dh, ssm_w, bsz, seq, tm, tk):
    t, d = x.shape
    aw = heads * dh
    n_lead = ssm_w + 3 * aw
    assert n_lead % LANES == 0
    w = w_in_all[layer]
    wf = jnp.pad(w[:, n_lead:n_lead + heads], ((0, 0), (0, LANES - heads))).astype(BF16)
    wga = w[:, n_lead + heads:n_lead + heads + d].astype(BF16)
    wgb = w[:, n_lead + heads + d:].astype(BF16)
    ws = [wf, wga, wgb]
    fb = jnp.pad(forget_b.reshape(1, heads), ((0, 0), (0, LANES - heads)))
    per_b = seq // tm
    row = lambda b, i: (b * per_b + i, 0)
    hp = heads * HEAD_PAD
    tok = lambda n, dt: (jax.ShapeDtypeStruct((t, n), dt), pl.BlockSpec((tm, n), row))
    vrows = 2 * (dh + V_EXTRA)
    vt = (jax.ShapeDtypeStruct((bsz, heads // 2, seq // tk, vrows, tk), BF16),
          pl.BlockSpec((None, heads // 2, tm // tk, vrows, tk), lambda b, i: (b, 0, i, 0, 0)))
    outs = [tok(ssm_w, F32), tok(hp, BF16), tok(hp, BF16), vt, tok(d, BF16), tok(d, BF16)]
    return pl.pallas_call(
        functools.partial(_inproj_kernel, heads=heads, dh=dh, tk=tk, ssm_w=ssm_w),
        out_shape=[o[0] for o in outs],
        grid=(bsz, per_b),
        in_specs=[pl.BlockSpec((tm, d), row),
                  pl.BlockSpec((1, 9, d), lambda b, i: (b, 0, 0)),
                  _const_spec((1, d)), _const_spec((1, LANES)),
                  _const_spec((d, n_lead))] + [_const_spec(a.shape) for a in ws],
        out_specs=[o[1] for o in outs],
        scratch_shapes=[pltpu.VMEM((1, LANES), F32)],
        compiler_params=_params(2),
    )(x, mod_l, g_pre.reshape(1, d), fb, w[:, :n_lead], *ws)


def _attn_kernel(q_ref, k_ref, vt_ref, o_ref, s_ref, mx_ref, m_ref, acc_ref, *, tq, tk, dh):
    i = pl.program_id(2)
    assert tq == 2 * tk
    visible = (lax.broadcasted_iota(jnp.int32, (tk, tk), 0)
               <= lax.broadcasted_iota(jnp.int32, (tk, tk), 1))
    head_cols = [slice(hh * HEAD_PAD, (hh + 1) * HEAD_PAD) for hh in range(2)]
    both, late = (0, 1), (1,)

    def scores(j, slot, halves=both, masked_half=None, hds=both):
        rows = pl.ds(pl.multiple_of(j * tk, tk), tk)
        for hh in hds:
            for hf in halves:
                q = q_ref[hf * tk:(hf + 1) * tk, head_cols[hh]]
                st = lax.dot_general(k_ref[rows, head_cols[hh]], q, (((1,), (1,)), ((), ())),
                                     preferred_element_type=F32)
                if hf == masked_half:
                    st = jnp.where(visible, st, -jnp.inf)
                s_ref[slot, hh, hf] = st
                mx_ref[slot, hh, hf] = jnp.max(st, axis=0, keepdims=True)

    vrows = dh + V_EXTRA

    def softmax_pv(j, slot, halves=both, hds=both):
        for hh in hds:
            for hf in halves:
                m = m_ref[hh, hf]
                m_new = jnp.maximum(m, mx_ref[slot, hh, hf])
                m_ref[hh, hf] = m_new
                p = jnp.exp2(s_ref[slot, hh, hf] - m_new)
                acc_ref[hh, hf] = jnp.exp2(m - m_new) * acc_ref[hh, hf] + _dot(
                    vt_ref[j, hh * vrows:(hh + 1) * vrows, :], p.astype(BF16))

    d0, d1 = 2 * i, 2 * i + 1
    m_ref[...] = jnp.full(m_ref.shape, -jnp.inf, F32)
    acc_ref[...] = jnp.zeros(acc_ref.shape, F32)
    scores(d0, 0, masked_half=0)

    def pair(t):
        blk0 = jnp.where(t == 0, d0, 2 * t - 1)
        for nxt, cur, slot in ((2 * t, blk0, 0), (2 * t + 1, 2 * t, 1)):
            for hh in both:
                for hf in both:
                    scores(nxt, 1 - slot, (hf,), hds=(hh,))
                    softmax_pv(cur, slot, (hf,), hds=(hh,))

    def two_pairs(u, carry):
        pair(2 * u)
        pair(2 * u + 1)
        return carry

    lax.fori_loop(0, lax.shift_right_logical(i, 1), two_pairs, 0)

    @pl.when((i & 1) == 1)
    def _():
        pair(i - 1)

    for hh in both:
        scores(d1, 1, late, masked_half=1, hds=(hh,))
        softmax_pv(jnp.where(i == 0, d0, 2 * i - 1), 0, hds=(hh,))
    softmax_pv(d1, 1, late)
    for hf in both:
        o_t = jnp.concatenate([acc_ref[hh, hf, :dh] / acc_ref[hh, hf, dh:dh + 1] for hh in range(2)],
                              axis=0)
        o_ref[hf * tk:(hf + 1) * tk, :] = o_t.T.astype(BF16)


def _attn_call(q_aug, k_aug, vt, *, heads, dh, bsz, seq, tq, tk):
    t = q_aug.shape[0]
    nq = seq // tq
    nkv = seq // tk
    pair = 2 * HEAD_PAD
    return pl.pallas_call(
        functools.partial(_attn_kernel, tq=tq, tk=tk, dh=dh),
        out_shape=jax.ShapeDtypeStruct((t, heads * dh), BF16),
        grid=(bsz, heads // 2, nq),
        in_specs=[pl.BlockSpec((tq, pair), lambda b, h, i: (b * nq + i, h)),
                  pl.BlockSpec((seq, pair), lambda b, h, i: (b, h)),
                  pl.BlockSpec((None, None, nkv, vt.shape[3], tk), lambda b, h, i: (b, h, 0, 0, 0))],
        out_specs=pl.BlockSpec((tq, 2 * dh), lambda b, h, i: (b * nq + i, h)),
        scratch_shapes=[pltpu.VMEM((2, 2, 2, tk, tk), F32),
                        pltpu.VMEM((2, 2, 2, 1, tk), F32),
                        pltpu.VMEM((2, 2, 1, tk), F32),
                        pltpu.VMEM((2, 2, dh + V_EXTRA, tk), F32)],
        compiler_params=_params(3),
    )(q_aug, k_aug, vt)


def _int_power(re, im, e, e_max):
    pr, pi = jnp.ones(e.shape, F32), jnp.zeros(e.shape, F32)
    bit = 0
    while (1 << bit) <= e_max:
        on = (lax.shift_right_logical(e, bit) & 1) == 1
        pr, pi = jnp.where(on, pr * re - pi * im, pr), jnp.where(on, pr * im + pi * re, pi)
        re, im = re * re - im * im, 2.0 * re * im
        bit += 1
    return pr, pi


def _s5_prep_kernel(re_c_ref, im_c_ref, re_r_ref, im_r_ref, ldt_ref, ct_re_ref, ct_im_ref,
                    bt_re_ref, bt_im_ref, d_ref,
                    mst_ref, ws_ref, wo_ref, a1_ref, a2_ref, b_scr, *, n_ch, p_st):
    cn = SSM_CHUNK * n_ch
    p2 = 2 * p_st
    gpb = LANES // n_ch
    assert cn == 2 * LANES and p2 == LANES
    shift_bits = n_ch.bit_length() - 1
    tau = lax.shift_right_logical(lax.broadcasted_iota(jnp.int32, (p2, cn), 1), shift_bits)
    top = lax.broadcasted_iota(jnp.int32, (p2, cn), 0) < p_st
    lane_b = lax.broadcasted_iota(jnp.int32, (n_ch, p2), 1) < p_st
    zr = lax.broadcasted_iota(jnp.int32, (n_ch, cn), 0)
    zl = lax.broadcasted_iota(jnp.int32, (n_ch, cn), 1)
    srow = lax.shift_right_logical(lax.broadcasted_iota(jnp.int32, (cn, p2), 0), shift_bits)
    lane_s = lax.broadcasted_iota(jnp.int32, (cn, p2), 1) < p_st
    lane_a = lax.broadcasted_iota(jnp.int32, (1, p2), 1) < p_st
    grp_z = lax.shift_right_logical(lax.broadcasted_iota(jnp.int32, (n_ch, LANES), 1), shift_bits)
    grp_r = lax.shift_right_logical(lax.broadcasted_iota(jnp.int32, (p2, LANES), 1), shift_bits)
    spread = (lax.broadcasted_iota(jnp.int32, (n_ch, cn), 0)
              == (lax.broadcasted_iota(jnp.int32, (n_ch, cn), 1) & (n_ch - 1))).astype(F32)

    ws_ref[0] = jnp.zeros(ws_ref.shape[1:], BF16)
    for g in range(gpb):
        dt = jnp.exp(ldt_ref[g])
        are_c = jnp.minimum(re_c_ref[g], A_RE_MAX) * dt
        aim_c = im_c_ref[g] * dt
        re_r = jnp.minimum(re_r_ref[g], A_RE_MAX)
        im_r = im_r_ref[g]
        are_r = re_r * dt
        aim_r = im_r * dt

        mag = jnp.exp(are_r)
        lb_re = mag * jnp.cos(aim_r)
        lb_im = mag * jnp.sin(aim_r)
        den = re_r * re_r + im_r * im_r
        xr = lb_re - 1.0
        coef_re = (xr * re_r + lb_im * im_r) / den
        coef_im = (lb_im * re_r - xr * im_r) / den

        ct_re = _dot(ct_re_ref[g], spread, HIGHEST)
        ct_im = _dot(ct_im_ref[g], spread, HIGHEST)

        mag_c = jnp.exp(are_c)
        lam_c = (mag_c * jnp.cos(aim_c), mag_c * jnp.sin(aim_c))
        pw_re, pw_im = _int_power(lam_c[0], lam_c[1], tau, SSM_CHUNK - 1)

        def times_c(pr, pi):
            rr = pr * ct_re - pi * ct_im
            ri = pr * ct_im + pi * ct_re
            return jnp.where(top, rr, -ri)

        r1 = times_c(pw_re * lam_c[0] - pw_im * lam_c[1], pw_re * lam_c[1] + pw_im * lam_c[0])
        bb_re = coef_re * bt_re_ref[g] - coef_im * bt_im_ref[g]
        bb_im = coef_re * bt_im_ref[g] + coef_im * bt_re_ref[g]
        z = _dot(jnp.where(lane_b, bb_re, bb_im), times_c(pw_re, pw_im), HIGHEST)
        z = z + jnp.where(zr == zl, d_ref[g], 0.0)

        pr, pi = _int_power(lb_re, lb_im, SSM_CHUNK - 1 - srow, SSM_CHUNK - 1)
        tb_re = jnp.concatenate([bb_re] * SSM_CHUNK, axis=0)
        tb_im = jnp.concatenate([bb_im] * SSM_CHUNK, axis=0)
        ws_g =jnp.where(lane_s, pr * tb_re - pi * tb_im, pr * tb_im + pi * tb_re).astype(BF16)

        for t in range(SSM_CHUNK):
            half = slice((t // gpb) * LANES, (t // gpb + 1) * LANES)
            shift = ((g - t % gpb) * n_ch) % LANES
            move = (lambda a: a) if shift == 0 else (lambda a: pltpu.roll(a, shift, axis=1))
            b_scr[t, g * n_ch:(g + 1) * n_ch, :] = jnp.where(grp_z == g, move(z[:, half]), 0.0)
            wo_ref[0, g * p2:(g + 1) * p2, t * LANES:(t + 1) * LANES] = (
                jnp.where(grp_r == g, move(r1[:, half]), 0.0).astype(BF16))
            ws_ref[0, t * LANES + g * n_ch:t * LANES + (g + 1) * n_ch, g * p2:(g + 1) * p2] = (
                ws_g[t * n_ch:(t + 1) * n_ch, :])

        ar, ai = lb_re, lb_im
        for _ in range(SSM_CHUNK.bit_length() - 1):
            ar, ai = ar * ar - ai * ai, 2.0 * ar * ai
        for k in range(SCAN_ROWS):
            a1_ref[0, k:k + 1, g * p2:(g + 1) * p2] = ar
            a2_ref[0, k:k + 1, g * p2:(g + 1) * p2] = jnp.where(lane_a, -ai, ai)
            ar, ai = ar * ar - ai * ai, 2.0 * ar * ai

    pairs = SSM_CHUNK // 2
    for d in range(pairs):
        r0 = (pairs - 1 - d) * 2 * LANES
        diag = b_scr[2 * d].astype(BF16)
        below = b_scr[2 * d - 1].astype(BF16) if d > 0 else jnp.zeros((LANES, LANES), BF16)
        mst_ref[0, r0:r0 + LANES, :LANES] = diag
        mst_ref[0, r0:r0 + LANES, LANES:] = b_scr[2 * d + 1].astype(BF16)
        mst_ref[0, r0 + LANES:r0 + 2 * LANES, :LANES] = below
        mst_ref[0, r0 + LANES:r0 + 2 * LANES, LANES:] = diag


def _s5_prep_call(a_re, a_im, log_dt, b_re, b_im, c_re, c_im, d_skip):
    p_st, n_ch = a_re.shape[-1], b_re.shape[-1]
    flat = lambda a, k: a.reshape((-1,) + a.shape[a.ndim - k:])
    a_re, a_im, log_dt = flat(a_re, 1), flat(a_im, 1), flat(log_dt, 0)
    b_re, b_im, c_re, c_im = flat(b_re, 2), flat(b_im, 2), flat(c_re, 2), flat(c_im, 2)
    g = a_re.shape[0]
    cn = SSM_CHUNK * n_ch
    p2 = 2 * p_st
    gpb = LANES // n_ch
    nblk = g // gpb
    assert g % gpb == 0
    col = lambda a: jnp.tile(a[:, :, None], (1, 2, 1))
    rowv = lambda a: jnp.tile(a[:, None, :], (1, 1, 2))
    ct = lambda a: jnp.tile(a.transpose(0, 2, 1), (1, 2, 1))
    bt = lambda a: jnp.tile(a.transpose(0, 2, 1), (1, 1, 2))
    d_pad = jnp.pad(d_skip.reshape(g, 1, n_ch), ((0, 0), (0, 0), (0, cn - n_ch)))
    args = (col(a_re), col(a_im), rowv(a_re), rowv(a_im), log_dt.reshape(g, 1, 1),
            ct(c_re), ct(c_im), bt(b_re), bt(b_im), d_pad)
    out_shapes = [jax.ShapeDtypeStruct((nblk, SSM_CHUNK * LANES, 2 * LANES), BF16),
                  jax.ShapeDtypeStruct((nblk, SSM_CHUNK * LANES, gpb * p2), BF16),
                  jax.ShapeDtypeStruct((nblk, gpb * p2, SSM_CHUNK * LANES), BF16),
                  jax.ShapeDtypeStruct((nblk, SCAN_ROWS, gpb * p2), F32),
                  jax.ShapeDtypeStruct((nblk, SCAN_ROWS, gpb * p2), F32)]
    return pl.pallas_call(
        functools.partial(_s5_prep_kernel, n_ch=n_ch, p_st=p_st),
        out_shape=out_shapes,
        grid=(nblk,),
        in_specs=[pl.BlockSpec((gpb,) + a.shape[1:], lambda i: (i, 0, 0)) for a in args],
        out_specs=[pl.BlockSpec((1,) + s.shape[1:], lambda i: (i, 0, 0)) for s in out_shapes],
        scratch_shapes=[pltpu.VMEM((SSM_CHUNK, LANES, LANES), F32)],
        compiler_params=_params(1),
    )(*args)


def _s5_kernel(u_ref, mst_ref, ws_ref, wo_ref, a1_ref, a2_ref, y_ref, xx_ref, sin_ref, yt_ref,
               *, p_st):
    ncb = xx_ref.shape[0]
    p2 = 2 * p_st
    pair = 2 * LANES
    pairs = SSM_CHUNK // 2
    for t in range(SSM_CHUNK):
        xx_ref[:, t * LANES:(t + 1) * LANES] = u_ref[pl.ds(t, ncb, stride=SSM_CHUNK), :].astype(BF16)
    s_all = _dot(xx_ref[...], ws_ref[0])
    for tp in range(pairs):
        yt_ref[:, tp * pair:(tp + 1) * pair] = _dot(xx_ref[:, :pair * (tp + 1)],
                                                    mst_ref[0, (pairs - 1 - tp) * pair:, :])
    row = lax.broadcasted_iota(jnp.int32, (ncb, p2), 0)
    for g in range(s_all.shape[1] // p2):
        cols = slice(g * p2, (g + 1) * p2)
        x = s_all[:, cols]
        d, k = 1, 0
        while d < ncb:
            xs = jnp.where(row >= d, pltpu.roll(x, d, axis=0), 0.0)
            x = x + a1_ref[0, k:k + 1, cols] * xs + a2_ref[0, k:k + 1, cols] * pltpu.roll(xs, p_st, axis=1)
            d, k = 2 * d, k + 1
        sin_ref[:, cols] = jnp.where(row >= 1, pltpu.roll(x, 1, axis=0), 0.0).astype(BF16)
    for tp in range(pairs):
        yy = (yt_ref[:, tp * pair:(tp + 1) * pair]
              + _dot(sin_ref[...], wo_ref[0, :, tp * pair:(tp + 1) * pair]))
        y_ref[pl.ds(2 * tp, ncb, stride=SSM_CHUNK), :] = yy[:, :LANES]
        y_ref[pl.ds(2 * tp + 1, ncb, stride=SSM_CHUNK), :] = yy[:, LANES:]


def _s5_call(u, mats, layer, *, bsz, seq):
    mst, ws, wo, a1, a2 = mats
    t, ssm_w = u.shape
    nblk = ssm_w // LANES
    ncb = seq // SSM_CHUNK
    p_st = LANES // 2
    assert ncb < 2 ** SCAN_ROWS
    wspec = lambda a: pl.BlockSpec((1,) + a.shape[1:], lambda o, b: (layer * nblk + o, 0, 0))
    return pl.pallas_call(
        functools.partial(_s5_kernel, p_st=p_st),
        out_shape=jax.ShapeDtypeStruct((t, ssm_w), F32),
        grid=(nblk, bsz),
        in_specs=[pl.BlockSpec((seq, LANES), lambda o, b: (b, o))] + [wspec(a) for a in mats],
        out_specs=pl.BlockSpec((seq, LANES), lambda o, b: (b, o)),
        scratch_shapes=[pltpu.VMEM((ncb, SSM_CHUNK * LANES), BF16),
                        pltpu.VMEM((ncb, ws.shape[2]), BF16),
                        pltpu.VMEM((ncb, SSM_CHUNK * LANES), F32)],
        compiler_params=_params(2),
    )(u, *mats)


def _merge_kernel(x_ref, mod_ref, gpost_ref, ys_ref, at_ref, ga_ref, gb_ref,
                  wglu_ref, wat_ref, wout_ref, o_ref):
    d = x_ref.shape[1]
    _, _, gate = _mod_rows(mod_ref, 1)
    z = _dot(jax.nn.gelu(ys_ref[...]).astype(BF16), wglu_ref[...].astype(BF16))
    y_a = z[:, :d] * jax.nn.sigmoid(z[:, d:])
    y_b = _dot(at_ref[...], wat_ref[...].astype(BF16))
    merged = ga_ref[...].astype(F32) * y_a + gb_ref[...].astype(F32) * y_b
    y = _dot(merged.astype(BF16), wout_ref[...].astype(BF16))
    o_ref[...] = x_ref[...] + gate * _rms(y, gpost_ref[...])


def _merge_call(x, mod_l, g_post, y_ssm, attn, ga, gb, glu_w, attn_w_out, w_out, layer, *, seq, tm):
    t, d = x.shape
    per_b = seq // tm
    row = lambda i: (i, 0)
    ws = [glu_w, attn_w_out, w_out]
    acts = (y_ssm, attn, ga, gb)
    return pl.pallas_call(
        _merge_kernel,
        out_shape=jax.ShapeDtypeStruct((t, d), F32),
        grid=(t // tm,),
        in_specs=[pl.BlockSpec((tm, d), row),
                  pl.BlockSpec((1, 9, d), lambda i: (i // per_b, 0, 0)),
                  _const_spec((1, d))]
                 + [pl.BlockSpec((tm, a.shape[1]), row) for a in acts]
                 + [pl.BlockSpec((None,) + a.shape[1:], lambda i: (layer, 0, 0),
                                 pipeline_mode=pl.Buffered(1)) for a in ws],
        out_specs=pl.BlockSpec((tm, d), row),
        compiler_params=_params(1),
    )(x, mod_l, g_post.reshape(1, d), *acts, *ws)


def kernel(x, c, mod_w, mod_b, norm_pre, norm_post, ffn_w_in, ffn_w_out, mix_w_in, forget_b,
           ssm_a_re, ssm_a_im, ssm_log_dt, ssm_b_re, ssm_b_im, ssm_c_re, ssm_c_im, ssm_d,
           glu_w, attn_w_out, mix_w_out):
    bsz, seq, d = x.shape
    depth = mod_w.shape[0]
    heads = forget_b.shape[1]
    dh = attn_w_out.shape[1] // heads
    groups = ssm_a_re.shape[1]
    ssm_w = ssm_d.shape[1]
    assert heads % 2 == 0 and 2 * dh == LANES and seq % SSM_CHUNK == 0
    tm = min(512, seq)
    tq = min(1024, seq)
    tk = tq // 2
    assert seq % tm == 0 and seq % tq == 0

    mod = _mod_call(c, mod_w, mod_b).reshape(depth, bsz, 9, d)
    mats = _s5_prep_call(ssm_a_re, ssm_a_im, ssm_log_dt, ssm_b_re, ssm_b_im, ssm_c_re, ssm_c_im, ssm_d)
    xt = x.reshape(bsz * seq, d)
    for l in range(depth):
        mod_l = mod[l]
        xt = _ffn_call(xt, mod_l, norm_pre[l, 0], norm_post[l, 0], ffn_w_in, ffn_w_out, l, 0,
                       sub=0, seq=seq, tm=tm)
        u, q_aug, k_aug, vt, ga, gb = _inproj_call(xt, mod_l, norm_pre[l, 1], mix_w_in, l, forget_b[l],
                                                   heads=heads, dh=dh, ssm_w=ssm_w, bsz=bsz, seq=seq,
                                                   tm=tm, tk=tk)
        attn = _attn_call(q_aug, k_aug, vt, heads=heads, dh=dh, bsz=bsz, seq=seq, tq=tq, tk=tk)
        y_ssm = _s5_call(u, mats, l, bsz=bsz, seq=seq)
        xt = _merge_call(xt, mod_l, norm_post[l, 1], y_ssm, attn, ga, gb, glu_w, attn_w_out,
                         mix_w_out, l, seq=seq, tm=tm)
        xt = _ffn_call(xt, mod_l, norm_pre[l, 2], norm_post[l, 2], ffn_w_in, ffn_w_out, l, 1,
                       sub=2, seq=seq, tm=tm)
    return xt.reshape(bsz, seq, d)
```

```python
import functools
import math

import jax
import jax.numpy as jnp
from jax import lax
from jax.experimental import pallas as pl
from jax.experimental.pallas import tpu as pltpu

F32 = jnp.float32
BF16 = jnp.bfloat16

RMS_EPS = 1e-6
FFN_RES = 0.5
A_RE_MAX = -1e-4
LOG2E = math.log2(math.e)
LANES = 128
HEAD_PAD = 128
V_EXTRA = 16
SSM_CHUNK = 16
SCAN_ROWS = 16
VMEM_LIMIT = 56 * 1024 * 1024
HIGHEST = lax.Precision.HIGHEST


def _dot(a, b, precision=None):
    return jnp.dot(a, b, preferred_element_type=F32, precision=precision)


def _rms(x, g):
    ms = jnp.mean(x * x, axis=-1, keepdims=True)
    return x * lax.rsqrt(ms + RMS_EPS) * g


def _silu(x):
    return x * jax.nn.sigmoid(x)


def _params(n_axes, semantics="arbitrary"):
    return pltpu.CompilerParams(dimension_semantics=(semantics,) * n_axes,
                                vmem_limit_bytes=VMEM_LIMIT)


def _const_spec(shape):
    zeros = (0,) * len(shape)
    return pl.BlockSpec(shape, lambda *_: zeros, pipeline_mode=pl.Buffered(1))


def _mod_kernel(c_ref, w_ref, b_ref, o_ref):
    sc = _silu(c_ref[...]).astype(BF16)
    o_ref[0] = _dot(sc, w_ref[0].astype(BF16)) + b_ref[0]


def _mod_call(c, mod_w, mod_b):
    depth, d, n = mod_w.shape
    bsz = c.shape[0]
    rows = 8
    c_pad = jnp.zeros((rows, d), F32).at[:bsz].set(c)
    tn = n // 8 if n % (8 * LANES) == 0 else n
    out = pl.pallas_call(
        _mod_kernel,
        out_shape=jax.ShapeDtypeStruct((depth, rows, n), F32),
        grid=(depth, n // tn),
        in_specs=[pl.BlockSpec((rows, d), lambda l, j: (0, 0)),
                  pl.BlockSpec((1, d, tn), lambda l, j: (l, 0, j)),
                  pl.BlockSpec((1, 1, tn), lambda l, j: (l, 0, j))],
        out_specs=pl.BlockSpec((1, rows, tn), lambda l, j: (l, 0, j)),
        compiler_params=_params(2),
    )(c_pad, mod_w, mod_b.reshape(depth, 1, n))
    return out[:, :bsz]


def _mod_rows(mod_ref, sub):
    shift = mod_ref[0, 3 * sub:3 * sub + 1, :]
    scale = mod_ref[0, 3 * sub + 1:3 * sub + 2, :]
    gate = mod_ref[0, 3 * sub + 2:3 * sub + 3, :]
    return shift, scale, gate


def _ffn_kernel(x_ref, mod_ref, gpre_ref, gpost_ref, wi_ref, wo_ref, o_ref, acc_ref, *, sub, tk):
    x = x_ref[...]
    shift, scale, gate = _mod_rows(mod_ref, sub)
    h = (_rms(x, gpre_ref[...]) * (1.0 + scale) + shift).astype(BF16)
    d_ff = wo_ref.shape[0]
    for k in range(d_ff // tk):
        g = _dot(h, wi_ref[:, k * tk:(k + 1) * tk].astype(BF16))
        u = _dot(h, wi_ref[:, d_ff + k * tk:d_ff + (k + 1) * tk].astype(BF16))
        a = (_silu(g) * u).astype(BF16)
        contrib = _dot(a, wo_ref[k * tk:(k + 1) * tk, :].astype(BF16))
        if k == 0:
            acc_ref[...] = contrib
        else:
            acc_ref[...] += contrib
    o_ref[...] = x + (FFN_RES * gate) * _rms(acc_ref[...], gpost_ref[...])


def _ffn_call(x, mod_l, g_pre, g_post, w_in_all, w_out_all, layer, which, *, sub, seq, tm):
    t, d = x.shape
    d_ff = w_out_all.shape[2]
    tk = 256 if d_ff % 256 == 0 else d_ff
    per_b = seq // tm
    row = lambda i: (i, 0)
    pick = lambda i: (layer, which, 0, 0)
    return pl.pallas_call(
        functools.partial(_ffn_kernel, sub=sub, tk=tk),
        out_shape=jax.ShapeDtypeStruct((t, d), F32),
        grid=(t // tm,),
        in_specs=[pl.BlockSpec((tm, d), row),
                  pl.BlockSpec((1, 9, d), lambda i: (i // per_b, 0, 0)),
                  _const_spec((1, d)), _const_spec((1, d)),
                  pl.BlockSpec((None, None, d, 2 * d_ff), pick, pipeline_mode=pl.Buffered(1)),
                  pl.BlockSpec((None, None, d_ff, d), pick, pipeline_mode=pl.Buffered(1))],
        out_specs=pl.BlockSpec((tm, d), row),
        scratch_shapes=[pltpu.VMEM((tm, d), F32)],
        compiler_params=_params(1),
    )(x, mod_l, g_pre.reshape(1, d), g_post.reshape(1, d), w_in_all, w_out_all)


def _split3(c):
    hi = c.astype(BF16).astype(F32)
    r = c - hi
    mid = r.astype(BF16).astype(F32)
    lo = (r - mid).astype(BF16).astype(F32)
    return hi, mid, lo


def _inproj_kernel(x_ref, mod_ref, gpre_ref, fb_ref, wl_ref, wf_ref, wga_ref, wgb_ref,
                   u_ref, q_ref, k_ref, vt_ref, ga_ref, gb_ref, carry_ref,
                   *, heads, dh, tk, ssm_w):
    tm = x_ref.shape[0]
    aw = heads * dh
    w_cols = lambda start, n: wl_ref[:, start:start + n].astype(BF16)

    @pl.when(pl.program_id(1) == 0)
    def _():
        carry_ref[...] = jnp.zeros_like(carry_ref)

    shift, scale, _ = _mod_rows(mod_ref, 1)
    h = (_rms(x_ref[...], gpre_ref[...]) * (1.0 + scale) + shift).astype(BF16)

    lane = lax.broadcasted_iota(jnp.int32, (tm, LANES), 1)
    log_f = jax.nn.log_sigmoid(_dot(h, wf_ref[...]) + fb_ref[...])
    hi, mid, lo = _split3(jnp.where(lane < heads, log_f, 0.0))
    packed = hi + pltpu.roll(mid, heads, axis=1) + pltpu.roll(lo, 2 * heads, axis=1)
    r = lax.broadcasted_iota(jnp.int32, (tm, tm), 0)
    c = lax.broadcasted_iota(jnp.int32, (tm, tm), 1)
    part = _dot((c <= r).astype(BF16), packed.astype(BF16))
    part = part + pltpu.roll(part, LANES - heads, axis=1) + pltpu.roll(part, LANES - 2 * heads, axis=1)
    cum = jnp.where(lane < heads, part, 0.0) + carry_ref[...]
    carry_ref[...] = cum[tm - 1:tm, :]

    q = _dot(h, w_cols(ssm_w, aw)) * (dh ** -0.5 * LOG2E)
    k = _dot(h, w_cols(ssm_w + aw, aw))
    ones_q = jnp.where((lane >= dh) & (lane < dh + 3), 1.0, 0.0)
    ones_k = jnp.where((lane >= dh + 3) & (lane < dh + 6), 1.0, 0.0)
    for hd in range(heads):
        hi, mid, lo = _split3(cum[:, hd:hd + 1] * LOG2E)
        q_aug = jnp.where(lane == dh + 3, hi, jnp.where(lane == dh + 4, mid,
                          jnp.where(lane == dh + 5, lo, ones_q)))
        k_aug = jnp.where(lane == dh, -hi, jnp.where(lane == dh + 1, -mid,
                          jnp.where(lane == dh + 2, -lo, ones_k)))
        src = slice((hd // 2) * LANES, (hd // 2 + 1) * LANES)
        move = (lambda a: a) if hd % 2 == 0 else (lambda a: pltpu.roll(a, dh, axis=1))
        dst = slice(hd * HEAD_PAD, (hd + 1) * HEAD_PAD)
        q_ref[hd] = jnp.where(lane < dh, move(q[:, src]), q_aug).T.astype(BF16)
        k_ref[:, dst] = jnp.where(lane < dh, move(k[:, src]), k_aug).astype(BF16)

    v_t = _dot(h, w_cols(ssm_w + 2 * aw, aw)).T.astype(BF16)
    ones_rows = (lax.broadcasted_iota(jnp.int32, (V_EXTRA, tk), 0) == 0).astype(BF16)
    for hd in range(heads):
        r0 = (hd % 2) * (dh + V_EXTRA)
        for blk in range(tm // tk):
            vt_ref[hd // 2, blk, r0:r0 + dh, :] = v_t[hd * dh:(hd + 1) * dh, blk * tk:(blk + 1) * tk]
            vt_ref[hd // 2, blk, r0 + dh:r0 + dh + V_EXTRA, :] = ones_rows

    u_ref[...] = _dot(h, w_cols(0, ssm_w))
    ga_ref[...] = jax.nn.sigmoid(_dot(h, wga_ref[...])).astype(BF16)
    gb_ref[...] = jax.nn.sigmoid(_dot(h, wgb_ref[...])).astype(BF16)


def _inproj_call(x, mod_l, g_pre, w_in_all, layer, forget_b, *, heads, dh, ssm_w, bsz, seq, tm, tk):
    t, d = x.shape
    aw = heads * dh
    n_lead = ssm_w + 3 * aw
    assert n_lead % LANES == 0
    w = w_in_all[layer]
    wf = jnp.pad(w[:, n_lead:n_lead + heads], ((0, 0), (0, LANES - heads))).astype(BF16)
    wga = w[:, n_lead + heads:n_lead + heads + d].astype(BF16)
    wgb = w[:, n_lead + heads + d:].astype(BF16)
    ws = [wf, wga, wgb]
    fb = jnp.pad(forget_b.reshape(1, heads), ((0, 0), (0, LANES - heads)))
    per_b = seq // tm
    row = lambda b, i: (b * per_b + i, 0)
    hp = heads * HEAD_PAD
    tok = lambda n, dt: (jax.ShapeDtypeStruct((t, n), dt), pl.BlockSpec((tm, n), row))
    vrows = 2 * (dh + V_EXTRA)
    vt = (jax.ShapeDtypeStruct((bsz, heads // 2, seq // tk, vrows, tk), BF16),
          pl.BlockSpec((None, heads // 2, tm // tk, vrows, tk), lambda b, i: (b, 0, i, 0, 0)))
    qt = (jax.ShapeDtypeStruct((bsz, heads, HEAD_PAD, seq), BF16),
          pl.BlockSpec((None, heads, HEAD_PAD, tm), lambda b, i: (b, 0, 0, i)))
    outs = [tok(ssm_w, F32), qt, tok(hp, BF16), vt, tok(d, BF16), tok(d, BF16)]
    return pl.pallas_call(
        functools.partial(_inproj_kernel, heads=heads, dh=dh, tk=tk, ssm_w=ssm_w),
        out_shape=[o[0] for o in outs],
        grid=(bsz, per_b),
        in_specs=[pl.BlockSpec((tm, d), row),
                  pl.BlockSpec((1, 9, d), lambda b, i: (b, 0, 0)),
                  _const_spec((1, d)), _const_spec((1, LANES)),
                  _const_spec((d, n_lead))] + [_const_spec(a.shape) for a in ws],
        out_specs=[o[1] for o in outs],
        scratch_shapes=[pltpu.VMEM((1, LANES), F32)],
        compiler_params=_params(2),
    )(x, mod_l, g_pre.reshape(1, d), fb, w[:, :n_lead], *ws)


def _attn_kernel(q_ref, k_ref, vt_ref, o_ref, s_ref, mx_ref, m_ref, acc_ref, *, tq, tk, dh):
    i = pl.program_id(2)
    assert tq == 2 * tk
    visible = (lax.broadcasted_iota(jnp.int32, (tk, tk), 0)
               <= lax.broadcasted_iota(jnp.int32, (tk, tk), 1))
    head_cols = [slice(hh * HEAD_PAD, (hh + 1) * HEAD_PAD) for hh in range(2)]
    both, late = (0, 1), (1,)

    def scores(j, slot, halves=both, masked_half=None, hds=both):
        rows = pl.ds(pl.multiple_of(j * tk, tk), tk)
        for hh in hds:
            for hf in halves:
                st = _dot(k_ref[rows, head_cols[hh]],
                          q_ref[hh, :, hf * tk:(hf + 1) * tk])
                if hf == masked_half:
                    st = jnp.where(visible, st, -jnp.inf)
                s_ref[slot, hh, hf] = st
                mx_ref[slot, hh, hf] = jnp.max(st, axis=0, keepdims=True)

    vrows = dh + V_EXTRA

    def softmax_pv(j, slot, halves=both, hds=both):
        for hh in hds:
            for hf in halves:
                m = m_ref[hh, hf]
                m_new = jnp.maximum(m, mx_ref[slot, hh, hf])
                m_ref[hh, hf] = m_new
                p = jnp.exp2(s_ref[slot, hh, hf] - m_new)
                acc_ref[hh, hf] = jnp.exp2(m - m_new) * acc_ref[hh, hf] + _dot(
                    vt_ref[j, hh * vrows:(hh + 1) * vrows, :], p.astype(BF16))

    d0, d1 = 2 * i, 2 * i + 1
    m_ref[...] = jnp.full(m_ref.shape, -jnp.inf, F32)
    acc_ref[...] = jnp.zeros(acc_ref.shape, F32)
    scores(d0, 0, masked_half=0)

    def pair(t):
        blk0 = jnp.where(t == 0, d0, 2 * t - 1)
        for nxt, cur, slot in ((2 * t, blk0, 0), (2 * t + 1, 2 * t, 1)):
            for hh in both:
                for hf in both:
                    scores(nxt, 1 - slot, (hf,), hds=(hh,))
                    softmax_pv(cur, slot, (hf,), hds=(hh,))

    def two_pairs(u, carry):
        pair(2 * u)
        pair(2 * u + 1)
        return carry

    lax.fori_loop(0, lax.shift_right_logical(i, 1), two_pairs, 0)

    @pl.when((i & 1) == 1)
    def _():
        pair(i - 1)

    for hh in both:
        scores(d1, 1, late, masked_half=1, hds=(hh,))
        softmax_pv(jnp.where(i == 0, d0, 2 * i - 1), 0, hds=(hh,))
    softmax_pv(d1, 1, late)
    for hf in both:
        o_t = jnp.concatenate([acc_ref[hh, hf, :dh] / acc_ref[hh, hf, dh:dh + 1] for hh in range(2)],
                              axis=0)
        o_ref[hf * tk:(hf + 1) * tk, :] = o_t.T.astype(BF16)


def _attn_call(q_aug, k_aug, vt, *, heads, dh, bsz, seq, tq, tk):
    t = bsz * seq
    nq = seq // tq
    nkv = seq // tk
    pair = 2 * HEAD_PAD
    return pl.pallas_call(
        functools.partial(_attn_kernel, tq=tq, tk=tk, dh=dh),
        out_shape=jax.ShapeDtypeStruct((t, heads * dh), BF16),
        grid=(bsz, heads // 2, nq),
        in_specs=[pl.BlockSpec((None, 2, HEAD_PAD, tq), lambda b, h, i: (b, h, 0, i)),
                  pl.BlockSpec((seq, pair), lambda b, h, i: (b, h)),
                  pl.BlockSpec((None, None, nkv, vt.shape[3], tk), lambda b, h, i: (b, h, 0, 0, 0))],
        out_specs=pl.BlockSpec((tq, 2 * dh), lambda b, h, i: (b * nq + i, h)),
        scratch_shapes=[pltpu.VMEM((2, 2, 2, tk, tk), F32),
                        pltpu.VMEM((2, 2, 2, 1, tk), F32),
                        pltpu.VMEM((2, 2, 1, tk), F32),
                        pltpu.VMEM((2, 2, dh + V_EXTRA, tk), F32)],
        compiler_params=_params(3),
    )(q_aug, k_aug, vt)


def _int_power(re, im, e, e_max):
    pr, pi = jnp.ones(e.shape, F32), jnp.zeros(e.shape, F32)
    bit = 0
    while (1 << bit) <= e_max:
        on = (lax.shift_right_logical(e, bit) & 1) == 1
        pr, pi = jnp.where(on, pr * re - pi * im, pr), jnp.where(on, pr * im + pi * re, pi)
        re, im = re * re - im * im, 2.0 * re * im
        bit += 1
    return pr, pi


def _s5_prep_kernel(re_c_ref, im_c_ref, re_r_ref, im_r_ref, ldt_ref, ct_re_ref, ct_im_ref,
                    bt_re_ref, bt_im_ref, d_ref,
                    mst_ref, ws_ref, wo_ref, a1_ref, a2_ref, b_scr, *, n_ch, p_st):
    cn = SSM_CHUNK * n_ch
    p2 = 2 * p_st
    gpb = LANES // n_ch
    assert cn == 2 * LANES and p2 == LANES
    shift_bits = n_ch.bit_length() - 1
    tau = lax.shift_right_logical(lax.broadcasted_iota(jnp.int32, (p2, cn), 1), shift_bits)
    top = lax.broadcasted_iota(jnp.int32, (p2, cn), 0) < p_st
    lane_b = lax.broadcasted_iota(jnp.int32, (n_ch, p2), 1) < p_st
    zr = lax.broadcasted_iota(jnp.int32, (n_ch, cn), 0)
    zl = lax.broadcasted_iota(jnp.int32, (n_ch, cn), 1)
    srow = lax.shift_right_logical(lax.broadcasted_iota(jnp.int32, (cn, p2), 0), shift_bits)
    lane_s = lax.broadcasted_iota(jnp.int32, (cn, p2), 1) < p_st
    lane_a = lax.broadcasted_iota(jnp.int32, (1, p2), 1) < p_st
    grp_z = lax.shift_right_logical(lax.broadcasted_iota(jnp.int32, (n_ch, LANES), 1), shift_bits)
    grp_r = lax.shift_right_logical(lax.broadcasted_iota(jnp.int32, (p2, LANES), 1), shift_bits)
    spread = (lax.broadcasted_iota(jnp.int32, (n_ch, cn), 0)
              == (lax.broadcasted_iota(jnp.int32, (n_ch, cn), 1) & (n_ch - 1))).astype(F32)

    ws_ref[0] = jnp.zeros(ws_ref.shape[1:], BF16)
    for g in range(gpb):
        dt = jnp.exp(ldt_ref[g])
        are_c = jnp.minimum(re_c_ref[g], A_RE_MAX) * dt
        aim_c = im_c_ref[g] * dt
        re_r = jnp.minimum(re_r_ref[g], A_RE_MAX)
        im_r = im_r_ref[g]
        are_r = re_r * dt
        aim_r = im_r * dt

        mag = jnp.exp(are_r)
        lb_re = mag * jnp.cos(aim_r)
        lb_im = mag * jnp.sin(aim_r)
        den = re_r * re_r + im_r * im_r
        xr = lb_re - 1.0
        coef_re = (xr * re_r + lb_im * im_r) / den
        coef_im = (lb_im * re_r - xr * im_r) / den

        ct_re = _dot(ct_re_ref[g], spread, HIGHEST)
        ct_im = _dot(ct_im_ref[g], spread, HIGHEST)

        mag_c = jnp.exp(are_c)
        lam_c = (mag_c * jnp.cos(aim_c), mag_c * jnp.sin(aim_c))
        pw_re, pw_im = _int_power(lam_c[0], lam_c[1], tau, SSM_CHUNK - 1)

        def times_c(pr, pi):
            rr = pr * ct_re - pi * ct_im
            ri = pr * ct_im + pi * ct_re
            return jnp.where(top, rr, -ri)

        r1 = times_c(pw_re * lam_c[0] - pw_im * lam_c[1], pw_re * lam_c[1] + pw_im * lam_c[0])
        bb_re = coef_re * bt_re_ref[g] - coef_im * bt_im_ref[g]
        bb_im = coef_re * bt_im_ref[g] + coef_im * bt_re_ref[g]
        z = _dot(jnp.where(lane_b, bb_re, bb_im), times_c(pw_re, pw_im), HIGHEST)
        z = z + jnp.where(zr == zl, d_ref[g], 0.0)

        pr, pi = _int_power(lb_re, lb_im, SSM_CHUNK - 1 - srow, SSM_CHUNK - 1)
        tb_re = jnp.concatenate([bb_re] * SSM_CHUNK, axis=0)
        tb_im = jnp.concatenate([bb_im] * SSM_CHUNK, axis=0)
        ws_g =jnp.where(lane_s, pr * tb_re - pi * tb_im, pr * tb_im + pi * tb_re).astype(BF16)

        for t in range(SSM_CHUNK):
            half = slice((t // gpb) * LANES, (t // gpb + 1) * LANES)
            shift = ((g - t % gpb) * n_ch) % LANES
            move = (lambda a: a) if shift == 0 else (lambda a: pltpu.roll(a, shift, axis=1))
            b_scr[t, g * n_ch:(g + 1) * n_ch, :] = jnp.where(grp_z == g, move(z[:, half]), 0.0)
            wo_ref[0, g * p2:(g + 1) * p2, t * LANES:(t + 1) * LANES] = (
                jnp.where(grp_r == g, move(r1[:, half]), 0.0).astype(BF16))
            ws_ref[0, t * LANES + g * n_ch:t * LANES + (g + 1) * n_ch, g * p2:(g + 1) * p2] = (
                ws_g[t * n_ch:(t + 1) * n_ch, :])

        ar, ai = lb_re, lb_im
        for _ in range(SSM_CHUNK.bit_length() - 1):
            ar, ai = ar * ar - ai * ai, 2.0 * ar * ai
        for k in range(SCAN_ROWS):
            a1_ref[0, k:k + 1, g * p2:(g + 1) * p2] = ar
            a2_ref[0, k:k + 1, g * p2:(g + 1) * p2] = jnp.where(lane_a, -ai, ai)
            ar, ai = ar * ar - ai * ai, 2.0 * ar * ai

    pairs = SSM_CHUNK // 2
    for d in range(pairs):
        r0 = (pairs - 1 - d) * 2 * LANES
        diag = b_scr[2 * d].astype(BF16)
        below = b_scr[2 * d - 1].astype(BF16) if d > 0 else jnp.zeros((LANES, LANES), BF16)
        mst_ref[0, r0:r0 + LANES, :LANES] = diag
        mst_ref[0, r0:r0 + LANES, LANES:] = b_scr[2 * d + 1].astype(BF16)
        mst_ref[0, r0 + LANES:r0 + 2 * LANES, :LANES] = below
        mst_ref[0, r0 + LANES:r0 + 2 * LANES, LANES:] = diag


def _s5_prep_call(a_re, a_im, log_dt, b_re, b_im, c_re, c_im, d_skip):
    p_st, n_ch = a_re.shape[-1], b_re.shape[-1]
    flat = lambda a, k: a.reshape((-1,) + a.shape[a.ndim - k:])
    a_re, a_im, log_dt = flat(a_re, 1), flat(a_im, 1), flat(log_dt, 0)
    b_re, b_im, c_re, c_im = flat(b_re, 2), flat(b_im, 2), flat(c_re, 2), flat(c_im, 2)
    g = a_re.shape[0]
    cn = SSM_CHUNK * n_ch
    p2 = 2 * p_st
    gpb = LANES // n_ch
    nblk = g // gpb
    assert g % gpb == 0
    col = lambda a: jnp.tile(a[:, :, None], (1, 2, 1))
    rowv = lambda a: jnp.tile(a[:, None, :], (1, 1, 2))
    ct = lambda a: jnp.tile(a.transpose(0, 2, 1), (1, 2, 1))
    bt = lambda a: jnp.tile(a.transpose(0, 2, 1), (1, 1, 2))
    d_pad = jnp.pad(d_skip.reshape(g, 1, n_ch), ((0, 0), (0, 0), (0, cn - n_ch)))
    args = (col(a_re), col(a_im), rowv(a_re), rowv(a_im), log_dt.reshape(g, 1, 1),
            ct(c_re), ct(c_im), bt(b_re), bt(b_im), d_pad)
    out_shapes = [jax.ShapeDtypeStruct((nblk, SSM_CHUNK * LANES, 2 * LANES), BF16),
                  jax.ShapeDtypeStruct((nblk, SSM_CHUNK * LANES, gpb * p2), BF16),
                  jax.ShapeDtypeStruct((nblk, gpb * p2, SSM_CHUNK * LANES), BF16),
                  jax.ShapeDtypeStruct((nblk, SCAN_ROWS, gpb * p2), F32),
                  jax.ShapeDtypeStruct((nblk, SCAN_ROWS, gpb * p2), F32)]
    return pl.pallas_call(
        functools.partial(_s5_prep_kernel, n_ch=n_ch, p_st=p_st),
        out_shape=out_shapes,
        grid=(nblk,),
        in_specs=[pl.BlockSpec((gpb,) + a.shape[1:], lambda i: (i, 0, 0)) for a in args],
        out_specs=[pl.BlockSpec((1,) + s.shape[1:], lambda i: (i, 0, 0)) for s in out_shapes],
        scratch_shapes=[pltpu.VMEM((SSM_CHUNK, LANES, LANES), F32)],
        compiler_params=_params(1),
    )(*args)


def _s5_kernel(u_ref, mst_ref, ws_ref, wo_ref, a1_ref, a2_ref, y_ref, xx_ref, sin_ref, yt_ref,
               *, p_st):
    ncb = xx_ref.shape[0]
    p2 = 2 * p_st
    pair = 2 * LANES
    pairs = SSM_CHUNK // 2
    for t in range(SSM_CHUNK):
        xx_ref[:, t * LANES:(t + 1) * LANES] = u_ref[pl.ds(t, ncb, stride=SSM_CHUNK), :].astype(BF16)
    s_all = _dot(xx_ref[...], ws_ref[0])
    for tp in range(pairs):
        yt_ref[:, tp * pair:(tp + 1) * pair] = _dot(xx_ref[:, :pair * (tp + 1)],
                                                    mst_ref[0, (pairs - 1 - tp) * pair:, :])
    row = lax.broadcasted_iota(jnp.int32, (ncb, p2), 0)
    for g in range(s_all.shape[1] // p2):
        cols = slice(g * p2, (g + 1) * p2)
        x = s_all[:, cols]
        d, k = 1, 0
        while d < ncb:
            xs = jnp.where(row >= d, pltpu.roll(x, d, axis=0), 0.0)
            x = x + a1_ref[0, k:k + 1, cols] * xs + a2_ref[0, k:k + 1, cols] * pltpu.roll(xs, p_st, axis=1)
            d, k = 2 * d, k + 1
        sin_ref[:, cols] = jnp.where(row >= 1, pltpu.roll(x, 1, axis=0), 0.0).astype(BF16)
    for tp in range(pairs):
        yy = (yt_ref[:, tp * pair:(tp + 1) * pair]
              + _dot(sin_ref[...], wo_ref[0, :, tp * pair:(tp + 1) * pair]))
        y_ref[pl.ds(2 * tp, ncb, stride=SSM_CHUNK), :] = yy[:, :LANES]
        y_ref[pl.ds(2 * tp + 1, ncb, stride=SSM_CHUNK), :] = yy[:, LANES:]


def _s5_call(u, mats, layer, *, bsz, seq):
    mst, ws, wo, a1, a2 = mats
    t, ssm_w = u.shape
    nblk = ssm_w // LANES
    ncb = seq // SSM_CHUNK
    p_st = LANES // 2
    assert ncb < 2 ** SCAN_ROWS
    wspec = lambda a: pl.BlockSpec((1,) + a.shape[1:], lambda o, b: (layer * nblk + o, 0, 0))
    return pl.pallas_call(
        functools.partial(_s5_kernel, p_st=p_st),
        out_shape=jax.ShapeDtypeStruct((t, ssm_w), F32),
        grid=(nblk, bsz),
        in_specs=[pl.BlockSpec((seq, LANES), lambda o, b: (b, o))] + [wspec(a) for a in mats],
        out_specs=pl.BlockSpec((seq, LANES), lambda o, b: (b, o)),
        scratch_shapes=[pltpu.VMEM((ncb, SSM_CHUNK * LANES), BF16),
                        pltpu.VMEM((ncb, ws.shape[2]), BF16),
                        pltpu.VMEM((ncb, SSM_CHUNK * LANES), F32)],
        compiler_params=_params(2),
    )(u, *mats)


def _merge_kernel(x_ref, mod_ref, gpost_ref, ys_ref, at_ref, ga_ref, gb_ref,
                  wglu_ref, wat_ref, wout_ref, o_ref):
    d = x_ref.shape[1]
    _, _, gate = _mod_rows(mod_ref, 1)
    z = _dot(jax.nn.gelu(ys_ref[...]).astype(BF16), wglu_ref[...].astype(BF16))
    y_a = z[:, :d] * jax.nn.sigmoid(z[:, d:])
    y_b = _dot(at_ref[...], wat_ref[...].astype(BF16))
    merged = ga_ref[...].astype(F32) * y_a + gb_ref[...].astype(F32) * y_b
    y = _dot(merged.astype(BF16), wout_ref[...].astype(BF16))
    o_ref[...] = x_ref[...] + gate * _rms(y, gpost_ref[...])


def _merge_call(x, mod_l, g_post, y_ssm, attn, ga, gb, glu_w, attn_w_out, w_out, layer, *, seq, tm):
    t, d = x.shape
    per_b = seq // tm
    row = lambda i: (i, 0)
    ws = [glu_w, attn_w_out, w_out]
    acts = (y_ssm, attn, ga, gb)
    return pl.pallas_call(
        _merge_kernel,
        out_shape=jax.ShapeDtypeStruct((t, d), F32),
        grid=(t // tm,),
        in_specs=[pl.BlockSpec((tm, d), row),
                  pl.BlockSpec((1, 9, d), lambda i: (i // per_b, 0, 0)),
                  _const_spec((1, d))]
                 + [pl.BlockSpec((tm, a.shape[1]), row) for a in acts]
                 + [pl.BlockSpec((None,) + a.shape[1:], lambda i: (layer, 0, 0),
                                 pipeline_mode=pl.Buffered(1)) for a in ws],
        out_specs=pl.BlockSpec((tm, d), row),
        compiler_params=_params(1),
    )(x, mod_l, g_post.reshape(1, d), *acts, *ws)


def kernel(x, c, mod_w, mod_b, norm_pre, norm_post, ffn_w_in, ffn_w_out, mix_w_in, forget_b,
           ssm_a_re, ssm_a_im, ssm_log_dt, ssm_b_re, ssm_b_im, ssm_c_re, ssm_c_im, ssm_d,
           glu_w, attn_w_out, mix_w_out):
    bsz, seq, d = x.shape
    depth = mod_w.shape[0]
    heads = forget_b.shape[1]
    dh = attn_w_out.shape[1] // heads
    groups = ssm_a_re.shape[1]
    ssm_w = ssm_d.shape[1]
    assert heads % 2 == 0 and 2 * dh == LANES and seq % SSM_CHUNK == 0
    tm = min(512, seq)
    tq = min(1024, seq)
    tk = tq // 2
    assert seq % tm == 0 and seq % tq == 0

    mod = _mod_call(c, mod_w, mod_b).reshape(depth, bsz, 9, d)
    mats = _s5_prep_call(ssm_a_re, ssm_a_im, ssm_log_dt, ssm_b_re, ssm_b_im, ssm_c_re, ssm_c_im, ssm_d)
    xt = x.reshape(bsz * seq, d)
    for l in range(depth):
        mod_l = mod[l]
        xt = _ffn_call(xt, mod_l, norm_pre[l, 0], norm_post[l, 0], ffn_w_in, ffn_w_out, l, 0,
                       sub=0, seq=seq, tm=tm)
        u, q_aug, k_aug, vt, ga, gb = _inproj_call(xt, mod_l, norm_pre[l, 1], mix_w_in, l, forget_b[l],
                                                   heads=heads, dh=dh, ssm_w=ssm_w, bsz=bsz, seq=seq,
                                                   tm=tm, tk=tk)
        attn = _attn_call(q_aug, k_aug, vt, heads=heads, dh=dh, bsz=bsz, seq=seq, tq=tq, tk=tk)
        y_ssm = _s5_call(u, mats, l, bsz=bsz, seq=seq)
        xt = _merge_call(xt, mod_l, norm_post[l, 1], y_ssm, attn, ga, gb, glu_w, attn_w_out,
                         mix_w_out, l, seq=seq, tm=tm)
        xt = _ffn_call(xt, mod_l, norm_pre[l, 2], norm_post[l, 2], ffn_w_in, ffn_w_out, l, 1,
                       sub=2, seq=seq, tm=tm)
    return xt.reshape(bsz, seq, d)
```
